```python
import functools
import jax, jax.numpy as jnp
from jax import lax
import numpy as np

D_MODEL = 4096
BATCH = 4
SEQ = 2048
DEPTH = 2
DEC_BATCH = 128
DEC_SEQ = 1
PAST_LEN = 16384
PAGE_SIZE = 128

N_META = 16
CHUNK = 128
EPS = 1e-6
N_RET = (DEPTH + 1) // 2
N_MLSTM = DEPTH // 2
RET_HEADS = 16
RET_DK = D_MODEL // RET_HEADS
RET_DV = 2 * D_MODEL // RET_HEADS
RET_VW = RET_HEADS * RET_DV
ROPE_BASE = 10000.0
M_INNER = 2 * D_MODEL
M_HEADS = 8
M_DK = D_MODEL // M_HEADS
M_DV = M_INNER // M_HEADS
CONV_W = 4
V_BLOCK = 4

kernel_name = 'retention_mlstm_hybrid_step'


def rmsnorm(x, g):
    x32 = x.astype(jnp.float32)
    y = x32 * lax.rsqrt(jnp.mean(x32 * x32, axis=-1, keepdims=True) + EPS)
    return (y * g.astype(jnp.float32)).astype(x.dtype)


def rope(x, pos):
    half = x.shape[-1] // 2
    inv = ROPE_BASE ** (-jnp.arange(half, dtype=jnp.float32) / half)
    ang = pos.astype(jnp.float32)[:, None] * inv[None, :]
    cos = jnp.cos(ang)[None, :, None, :]
    sin = jnp.sin(ang)[None, :, None, :]
    x1, x2 = x[..., :half], x[..., half:]
    return jnp.concatenate([x1 * cos - x2 * sin, x1 * sin + x2 * cos], axis=-1)


def run_chunks(chunk_fn, state, xs, lead):
    outs = []
    if lead > 0:
        state, o = chunk_fn(state, tuple(a[:, :lead] for a in xs))
        outs.append(o)
        xs = tuple(a[:, lead:] for a in xs)
    b, t = xs[0].shape[0], xs[0].shape[1]
    size = CHUNK if t % CHUNK == 0 else t
    n = t // size
    xs_c = tuple(jnp.moveaxis(a.reshape((b, n, size) + a.shape[2:]), 1, 0) for a in xs)
    state, o = lax.scan(chunk_fn, state, xs_c)
    outs.append(jnp.moveaxis(o, 0, 1).reshape((b, t) + o.shape[3:]))
    return state, jnp.concatenate(outs, axis=1)


def retention_chunk(s, inp, log_g):
    q, k, v = inp
    L = q.shape[1]
    idx = jnp.arange(L, dtype=jnp.float32)
    diff = idx[:, None] - idx[None, :]
    causal = diff >= 0
    decay = jnp.where(causal[None], jnp.exp(log_g[:, None, None] * jnp.where(causal, diff, 0.0)[None]), 0.0)
    scores = jnp.einsum('blhd,bmhd->bhlm', q, k) * decay[None]
    inner = jnp.einsum('bhlm,bmhe->blhe', scores, v)
    q_dec = jnp.exp(log_g[None, :] * (idx[:, None] + 1.0))
    cross = jnp.einsum('blhd,bhde->blhe', q, s) * q_dec[None, :, :, None]
    k_dec = jnp.exp(log_g[None, :] * (L - 1.0 - idx)[:, None])
    s_new = s * jnp.exp(log_g * L)[None, :, None, None] + jnp.einsum('blhd,blhe->bhde', k * k_dec[None, :, :, None], v)
    return s_new, inner + cross


def mlstm_chunk(state, inp):
    c, n, m = state
    q, k, v, logi, logf = inp
    L = q.shape[1]
    bcum = jnp.cumsum(logf, axis=1)
    causal = jnp.tril(jnp.ones((L, L), dtype=bool))
    dlog = bcum[:, :, None, :] - bcum[:, None, :, :] + logi[:, None, :, :]
    dlog = jnp.where(causal[None, :, :, None], dlog, -jnp.inf)
    inter = bcum + m[:, None, :]
    m_row = jnp.maximum(inter, jnp.max(dlog, axis=2))
    w = jnp.exp(dlog - m_row[:, :, None, :])
    s_inter = jnp.exp(inter - m_row)
    qk = jnp.einsum('bihd,bjhd->bijh', q, k) * w
    num = jnp.einsum('bijh,bjhe->bihe', qk, v) + s_inter[..., None] * jnp.einsum('bihd,bhde->bihe', q, c)
    den = jnp.sum(qk, axis=2) + s_inter * jnp.einsum('bihd,bhd->bih', q, n)
    h = num / jnp.maximum(jnp.abs(den), jnp.exp(-m_row))[..., None]
    m_new = m_row[:, -1]
    wk = jnp.exp(bcum[:, -1:, :] - bcum + logi - m_new[:, None, :])
    s_c = jnp.exp(bcum[:, -1] + m - m_new)
    c_new = s_c[..., None, None] * c + jnp.einsum('bjhd,bjhe->bhde', k * wk[..., None], v)
    n_new = s_c[..., None] * n + jnp.einsum('bjhd,bjh->bhd', k, wk)
    return (c_new, n_new, m_new), h


def retention_layer(x, s0, pos, lead, w_in, w_out):
    b, t, _ = x.shape
    q, k, v, g = jnp.split(x @ w_in, [D_MODEL, 2 * D_MODEL, 2 * D_MODEL + RET_VW], axis=-1)
    q = rope(q.reshape(b, t, RET_HEADS, RET_DK).astype(jnp.float32), pos)
    k = rope(k.reshape(b, t, RET_HEADS, RET_DK).astype(jnp.float32), pos) * (RET_DK ** -0.5)
    v = v.reshape(b, t, RET_HEADS, RET_DV).astype(jnp.float32)
    log_g = jnp.log1p(-jnp.exp2(-5.0 - jnp.arange(RET_HEADS, dtype=jnp.float32)))
    s, o = run_chunks(functools.partial(retention_chunk, log_g=log_g), s0.astype(jnp.float32), (q, k, v), lead)
    o = o * lax.rsqrt(jnp.mean(o * o, axis=-1, keepdims=True) + EPS)
    y = (jax.nn.silu(g) * o.reshape(b, t, RET_VW).astype(x.dtype)) @ w_out
    return y, s


def mlstm_layer(x, c0, n0, m0, buf, lead, w_in, conv_w, conv_b, w_q, w_k, w_v, w_if, b_if, skip, gn_g, w_out):
    b, t, _ = x.shape
    xm, z, o_pre = jnp.split(x @ w_in, [M_INNER, 2 * M_INNER], axis=-1)
    xpad = jnp.concatenate([buf.astype(xm.dtype), xm], axis=1)
    xc = jax.nn.silu(conv_b + sum(xpad[:, i:i + t] * conv_w[i] for i in range(CONV_W)))
    new_buf = xpad[:, t:]
    q = xc @ w_q
    k = xc @ w_k
    v = jnp.einsum('btnc,ncd->btnd', xm.reshape(b, t, M_INNER // V_BLOCK, V_BLOCK), w_v).reshape(b, t, M_INNER)
    gates = (jnp.concatenate([q, k, v], axis=-1) @ w_if + b_if).astype(jnp.float32)
    logi, fpre = jnp.split(gates, 2, axis=-1)
    logf = jax.nn.log_sigmoid(fpre)
    qh = q.reshape(b, t, M_HEADS, M_DK).astype(jnp.float32)
    kh = k.reshape(b, t, M_HEADS, M_DK).astype(jnp.float32) * (M_DK ** -0.5)
    vh = v.reshape(b, t, M_HEADS, M_DV).astype(jnp.float32)
    state0 = (c0.astype(jnp.float32), n0.astype(jnp.float32), m0.astype(jnp.float32))
    (c, n, m), h = run_chunks(mlstm_chunk, state0, (qh, kh, vh, logi, logf), lead)
    h = jax.nn.sigmoid(o_pre.astype(jnp.float32)).reshape(b, t, M_HEADS, M_DV) * h
    mu = jnp.mean(h, axis=-1, keepdims=True)
    var = jnp.mean(jnp.square(h - mu), axis=-1, keepdims=True)
    h = ((h - mu) * lax.rsqrt(var + EPS)).reshape(b, t, M_INNER) * gn_g.astype(jnp.float32)
    out = (h.astype(x.dtype) + skip * xc) * jax.nn.silu(z)
    return out @ w_out, (c, n, m, new_buf)


def setup_inputs(seed: int = 0) -> dict:
    key = jax.random.key(seed)
    ks = jax.random.split(key, 24)
    f32 = jnp.float32

    def nrm(k, shape, scale):
        return jax.random.normal(k, shape, f32) * scale

    x_prompt = nrm(ks[0], (BATCH, SEQ, D_MODEL), 1.0)
    x_sample = nrm(ks[1], (DEC_BATCH, DEC_SEQ, D_MODEL), 1.0)
    state_ret_S = nrm(ks[2], (N_RET, DEC_BATCH, RET_HEADS, RET_DK, RET_DV), 0.1)
    state_mlstm_C = nrm(ks[3], (N_MLSTM, DEC_BATCH, M_HEADS, M_DK, M_DV), 0.05)
    state_mlstm_n = nrm(ks[4], (N_MLSTM, DEC_BATCH, M_HEADS, M_DK), 0.05)
    state_mlstm_m = jax.random.uniform(ks[5], (N_MLSTM, DEC_BATCH, M_HEADS), f32, 0.0, 3.0)
    state_mlstm_conv = nrm(ks[6], (N_MLSTM, DEC_BATCH, CONV_W - 1, M_INNER), 1.0)
    meta_tokens = nrm(ks[7], (N_META, D_MODEL), 1.0)
    norm_g = 1.0 + nrm(ks[8], (DEPTH, D_MODEL), 0.02)
    final_norm_g = 1.0 + nrm(ks[9], (D_MODEL,), 0.02)
    ret_w_in = nrm(ks[10], (N_RET, D_MODEL, 2 * D_MODEL + 2 * RET_VW), D_MODEL ** -0.5)
    ret_w_out = nrm(ks[11], (N_RET, RET_VW, D_MODEL), RET_VW ** -0.5)
    ml_w_in = nrm(ks[12], (N_MLSTM, D_MODEL, 3 * M_INNER), D_MODEL ** -0.5)
    ml_conv_w = nrm(ks[13], (N_MLSTM, CONV_W, M_INNER), CONV_W ** -0.5)
    ml_conv_b = nrm(ks[14], (N_MLSTM, M_INNER), 0.01)
    ml_w_q = nrm(ks[15], (N_MLSTM, M_INNER, M_HEADS * M_DK), M_INNER ** -0.5)
    ml_w_k = nrm(ks[16], (N_MLSTM, M_INNER, M_HEADS * M_DK), M_INNER ** -0.5)
    ml_w_v = nrm(ks[17], (N_MLSTM, M_INNER // V_BLOCK, V_BLOCK, V_BLOCK), V_BLOCK ** -0.5)
    gate_in = 2 * M_HEADS * M_DK + M_INNER
    ml_w_if = nrm(ks[18], (N_MLSTM, gate_in, 2 * M_HEADS), gate_in ** -0.5)
    b_i = nrm(ks[19], (N_MLSTM, M_HEADS), 0.1)
    b_f = jnp.linspace(3.0, 6.0, M_HEADS, dtype=f32)[None, :] + nrm(ks[20], (N_MLSTM, M_HEADS), 0.1)
    ml_b_if = jnp.concatenate([b_i, b_f], axis=-1)
    ml_skip = 1.0 + nrm(ks[21], (N_MLSTM, M_INNER), 0.02)
    ml_gn_g = 1.0 + nrm(ks[22], (N_MLSTM, M_INNER), 0.02)
    ml_w_out = nrm(ks[23], (N_MLSTM, M_INNER, D_MODEL), M_INNER ** -0.5)
    return {'x_prompt': x_prompt, 'x_sample': x_sample,
            'state_ret_S': state_ret_S, 'state_mlstm_C': state_mlstm_C, 'state_mlstm_n': state_mlstm_n,
            'state_mlstm_m': state_mlstm_m, 'state_mlstm_conv': state_mlstm_conv,
            'meta_tokens': meta_tokens, 'norm_g': norm_g, 'final_norm_g': final_norm_g,
            'ret_w_in': ret_w_in, 'ret_w_out': ret_w_out,
            'ml_w_in': ml_w_in, 'ml_conv_w': ml_conv_w, 'ml_conv_b': ml_conv_b,
            'ml_w_q': ml_w_q, 'ml_w_k': ml_w_k, 'ml_w_v': ml_w_v, 'ml_w_if': ml_w_if, 'ml_b_if': ml_b_if,
            'ml_skip': ml_skip, 'ml_gn_g': ml_gn_g, 'ml_w_out': ml_w_out}


def reference(x_prompt, x_sample, state_ret_S, state_mlstm_C, state_mlstm_n, state_mlstm_m, state_mlstm_conv,
              meta_tokens, norm_g, final_norm_g, ret_w_in, ret_w_out,
              ml_w_in, ml_conv_w, ml_conv_b, ml_w_q, ml_w_k, ml_w_v, ml_w_if, ml_b_if, ml_skip, ml_gn_g, ml_w_out):
    f32 = jnp.float32
    bp = x_prompt.shape[0]
    hp = jnp.concatenate([jnp.broadcast_to(meta_tokens[None].astype(x_prompt.dtype), (bp, N_META, D_MODEL)), x_prompt], axis=1)
    hs = x_sample
    pos_p = jnp.arange(hp.shape[1], dtype=jnp.int32)
    pos_s = PAST_LEN + jnp.arange(x_sample.shape[1], dtype=jnp.int32)
    ret_p, ret_s, c_p, c_s, n_p, n_s, m_p, m_s, cv_p, cv_s = ([] for _ in range(10))
    for i in range(DEPTH):
        j = i // 2
        xp = rmsnorm(hp, norm_g[i])
        xs = rmsnorm(hs, norm_g[i])
        if i % 2 == 0:
            s0 = jnp.zeros((bp, RET_HEADS, RET_DK, RET_DV), f32)
            yp, sp = retention_layer(xp, s0, pos_p, N_META, ret_w_in[j], ret_w_out[j])
            ys, ss = retention_layer(xs, state_ret_S[j], pos_s, 0, ret_w_in[j], ret_w_out[j])
            ret_p.append(sp.astype(x_prompt.dtype))
            ret_s.append(ss.astype(state_ret_S.dtype))
        else:
            w = (ml_w_in[j], ml_conv_w[j], ml_conv_b[j], ml_w_q[j], ml_w_k[j], ml_w_v[j], ml_w_if[j], ml_b_if[j],
                 ml_skip[j], ml_gn_g[j], ml_w_out[j])
            c0 = jnp.zeros((bp, M_HEADS, M_DK, M_DV), f32)
            n0 = jnp.zeros((bp, M_HEADS, M_DK), f32)
            m0 = jnp.zeros((bp, M_HEADS), f32)
            b0 = jnp.zeros((bp, CONV_W - 1, M_INNER), hp.dtype)
            yp, (cp_, np_, mp_, bp_) = mlstm_layer(xp, c0, n0, m0, b0, N_META, *w)
            ys, (cs_, ns_, ms_, bs_) = mlstm_layer(xs, state_mlstm_C[j], state_mlstm_n[j], state_mlstm_m[j],
                                                  state_mlstm_conv[j], 0, *w)
            c_p.append(cp_.astype(x_prompt.dtype)); c_s.append(cs_.astype(state_mlstm_C.dtype))
            n_p.append(np_.astype(x_prompt.dtype)); n_s.append(ns_.astype(state_mlstm_n.dtype))
            m_p.append(mp_.astype(x_prompt.dtype)); m_s.append(ms_.astype(state_mlstm_m.dtype))
            cv_p.append(bp_.astype(x_prompt.dtype)); cv_s.append(bs_.astype(state_mlstm_conv.dtype))
        hp = hp + yp.astype(hp.dtype)
        hs = hs + ys.astype(hs.dtype)
    y_prompt = rmsnorm(hp, final_norm_g)[:, N_META:]
    y_sample = rmsnorm(hs, final_norm_g)
    return (y_prompt, y_sample, jnp.stack(ret_p), jnp.stack(ret_s), jnp.stack(c_p), jnp.stack(c_s),
            jnp.stack(n_p), jnp.stack(n_s), jnp.stack(m_p), jnp.stack(m_s), jnp.stack(cv_p), jnp.stack(cv_s))
```

```python
import functools

import jax
import jax.numpy as jnp
from jax import lax
from jax.experimental import pallas as pl
from jax.experimental.pallas import tpu as pltpu

F32 = jnp.float32
BF16 = jnp.bfloat16

N_META = 16
CHUNK = 128
EPS = 1e-6
RET_HEADS = 16
M_HEADS = 8
CONV_W = 4
V_BLOCK = 4
ROPE_BASE = 10000.0
PAST_LEN = 16384

ROW_ALIGN = 256
MIB = 1024 * 1024
NT_DIMS = (((1,), (1,)), ((), ()))
TN_DIMS = (((0,), (0,)), ((), ()))


def _cparams(sem, vmem_mib=None):
    kw = dict(dimension_semantics=sem)
    if vmem_mib is not None:
        kw["vmem_limit_bytes"] = int(vmem_mib * MIB)
    return pltpu.CompilerParams(**kw)


def _pick(n, candidates):
    for c in candidates:
        if n % c == 0:
            return c
    raise ValueError(f"no tile for {n} in {candidates}")


def _sigmoid(x):
    return 1.0 / (1.0 + jnp.exp(-x))


def _silu(x):
    return x * _sigmoid(x)


def _rmsnorm_kernel(x_ref, g_ref, o_ref):
    x = x_ref[...]
    y = x * lax.rsqrt(jnp.mean(x * x, axis=-1, keepdims=True) + EPS)
    o_ref[...] = (y * g_ref[...]).astype(o_ref.dtype)


def _rmsnorm(x, g, *, out_dtype, block_rows, first_block=0, n_blocks=None):
    m, d = x.shape
    if n_blocks is None:
        n_blocks = m // block_rows
    return pl.pallas_call(
        _rmsnorm_kernel,
        grid=(n_blocks,),
        in_specs=[pl.BlockSpec((block_rows, d), lambda i: (i + first_block, 0)),
                  pl.BlockSpec((1, d), lambda i: (0, 0))],
        out_specs=pl.BlockSpec((block_rows, d), lambda i: (i, 0)),
        out_shape=jax.ShapeDtypeStruct((n_blocks * block_rows, d), out_dtype),
        compiler_params=_cparams(("parallel",)),
        name="rmsnorm",
    )(x, g.reshape(1, d))


def _matmul_kernel(x_ref, w_ref, *rest, rope, has_res):
    pos = 0
    if rope is not None:
        cos_ref, sin_ref = rest[0], rest[1]
        pos = 2
    if has_res:
        res_ref = rest[pos]
        pos += 1
    o_ref, wb_ref = rest[pos], rest[pos + 1]
    j = pl.program_id(0)

    @pl.when(pl.program_id(1) == 0)
    def _():
        wb_ref[...] = w_ref[...].astype(BF16)

    acc = jnp.dot(x_ref[...], wb_ref[...], preferred_element_type=F32)
    if has_res:
        acc = res_ref[...] + acc
    if rope is None:
        o_ref[...] = acc.astype(o_ref.dtype)
        return

    n_q, n_k, head_dim, k_scale = rope
    half = head_dim // 2
    bn = o_ref.shape[1]

    @pl.when(j < n_q + n_k)
    def _():
        scale = jnp.where(j < n_q, 1.0, k_scale).astype(F32)
        cos = cos_ref[...]
        sin = sin_ref[...]
        for h in range(bn // head_dim):
            lo = h * head_dim
            x1 = acc[:, lo:lo + half]
            x2 = acc[:, lo + half:lo + head_dim]
            o_ref[:, lo:lo + half] = ((x1 * cos - x2 * sin) * scale).astype(o_ref.dtype)
            o_ref[:, lo + half:lo + head_dim] = ((x1 * sin + x2 * cos) * scale).astype(o_ref.dtype)

    @pl.when(j >= n_q + n_k)
    def _():
        o_ref[...] = acc.astype(o_ref.dtype)


def _matmul(x, w, *, bm, bn, out_dtype=F32, res=None, rope=None, cos=None, sin=None,
            w_single_buffer=False, vmem_mib=48, name="matmul"):
    m, k = x.shape
    _, n = w.shape
    grid = (n // bn, m // bm)
    w_kw = dict(pipeline_mode=pl.Buffered(1)) if w_single_buffer else {}
    in_specs = [pl.BlockSpec((bm, k), lambda j, i: (i, 0)),
                pl.BlockSpec((k, bn), lambda j, i: (0, j), **w_kw)]
    args = [x, w]
    if rope is not None:
        half = rope[2] // 2
        in_specs += [pl.BlockSpec((bm, half), lambda j, i: (i, 0)),
                     pl.BlockSpec((bm, half), lambda j, i: (i, 0))]
        args += [cos, sin]
    if res is not None:
        in_specs.append(pl.BlockSpec((bm, bn), lambda j, i: (i, j)))
        args.append(res)
    return pl.pallas_call(
        functools.partial(_matmul_kernel, rope=rope, has_res=res is not None),
        grid=grid,
        in_specs=in_specs,
        out_specs=pl.BlockSpec((bm, bn), lambda j, i: (i, j)),
        out_shape=jax.ShapeDtypeStruct((m, n), out_dtype),
        scratch_shapes=[pltpu.VMEM((k, bn), BF16)],
        compiler_params=_cparams(("arbitrary", "arbitrary"), vmem_mib),
        name=name,
    )(*args)


def _ret_head_out(o, g):
    o = o * lax.rsqrt(jnp.mean(o * o, axis=-1, keepdims=True) + EPS)
    return _silu(g) * o


def _ret_chunk_kernel(lg_ref, q_ref, k_ref, v_ref, g_ref, *rest, has_s0):
    if has_s0:
        s0_ref, _, o_ref, s_ref = rest
    else:
        _, o_ref, s_ref = rest
    c = pl.program_id(2)

    @pl.when(c == 0)
    def _():
        if has_s0:
            s_ref[...] = s0_ref[...]
        else:
            s_ref[...] = jnp.zeros_like(s_ref)

    L = q_ref.shape[0]
    lg = jnp.full((1, 1), lg_ref[pl.program_id(1)], F32)
    q = q_ref[...].astype(BF16)
    k = k_ref[...]
    v = v_ref[...].astype(BF16)
    s = s_ref[0, 0]

    ii = lax.broadcasted_iota(jnp.int32, (L, L), 0)
    jj = lax.broadcasted_iota(jnp.int32, (L, L), 1)
    causal = ii >= jj
    diff = (ii - jj).astype(F32)
    decay = jnp.where(causal, jnp.exp(lg * jnp.where(causal, diff, 0.0)), 0.0)
    scores = lax.dot_general(q, k.astype(BF16), NT_DIMS, preferred_element_type=F32) * decay
    inner = jnp.dot(scores.astype(BF16), v, preferred_element_type=F32)
    idx = lax.broadcasted_iota(jnp.int32, (L, 1), 0).astype(F32)
    q_dec = jnp.exp(lg * (idx + 1.0))
    cross = jnp.dot(q, s.astype(BF16), preferred_element_type=F32) * q_dec
    k_dec = jnp.exp(lg * (L - 1.0 - idx))
    kd = (k * k_dec).astype(BF16)
    s_ref[0, 0] = s * jnp.exp(lg * float(L)) + lax.dot_general(kd, v, TN_DIMS, preferred_element_type=F32)
    o_ref[...] = _ret_head_out(inner + cross, g_ref[...]).astype(o_ref.dtype)


def _ret_chunks(log_g, qkvg, x_out, s0, *, batch, heads, dk, dv, L, n_chunks, row0, name):
    rb0 = row0 // L
    k_off = heads * dk // dk
    v_off = 2 * heads * dk // dv
    g_off = v_off + heads
    m = qkvg.shape[0]

    def rows(b, h, c):
        return rb0 + b * n_chunks + c

    in_specs = [
        pl.BlockSpec(memory_space=pltpu.SMEM),
        pl.BlockSpec((L, dk), lambda b, h, c: (rows(b, h, c), h)),
        pl.BlockSpec((L, dk), lambda b, h, c: (rows(b, h, c), k_off + h)),
        pl.BlockSpec((L, dv), lambda b, h, c: (rows(b, h, c), v_off + h)),
        pl.BlockSpec((L, dv), lambda b, h, c: (rows(b, h, c), g_off + h)),
    ]
    args = [log_g, qkvg, qkvg, qkvg, qkvg]
    if s0 is not None:
        in_specs.append(pl.BlockSpec((1, 1, dk, dv), lambda b, h, c: (b, h, 0, 0)))
        args.append(s0)
    in_specs.append(pl.BlockSpec(memory_space=pl.ANY))
    args.append(x_out)
    return pl.pallas_call(
        functools.partial(_ret_chunk_kernel, has_s0=s0 is not None),
        grid=(batch, heads, n_chunks),
        in_specs=in_specs,
        out_specs=[pl.BlockSpec((L, dv), lambda b, h, c: (rows(b, h, c), h)),
                   pl.BlockSpec((1, 1, dk, dv), lambda b, h, c: (b, h, 0, 0))],
        out_shape=[jax.ShapeDtypeStruct(x_out.shape, x_out.dtype),
                   jax.ShapeDtypeStruct((batch, heads, dk, dv), F32)],
        input_output_aliases={len(args) - 1: 0},
        compiler_params=_cparams(("parallel", "parallel", "arbitrary")),
        name=name,
    )(*args)


def _select_lane(x, lane_onehot):
    return jnp.sum(jnp.where(lane_onehot, x, 0.0), axis=1, keepdims=True)


def _ret_decode_kernel(lg_ref, q_ref, k_ref, v_ref, g_ref, s_ref, o_ref, so_ref, qt_ref, kt_ref,
                       *, hpg, dk, dv):
    hg = pl.program_id(0)
    b = pl.program_id(1)

    @pl.when(b == 0)
    def _():
        qt_ref[...] = q_ref[...].T
        kt_ref[...] = k_ref[...].T

    n_dec = q_ref.shape[0]
    onehot = lax.broadcasted_iota(jnp.int32, (1, n_dec), 1) == b
    for hh in range(hpg):
        gamma = jnp.exp(jnp.full((1, 1), lg_ref[hg * hpg + hh], F32))
        kcol = _select_lane(kt_ref[hh * dk:(hh + 1) * dk, :], onehot)
        qcol = _select_lane(qt_ref[hh * dk:(hh + 1) * dk, :], onehot)
        vrow = v_ref[pl.ds(b, 1), hh * dv:(hh + 1) * dv]
        grow = g_ref[pl.ds(b, 1), hh * dv:(hh + 1) * dv]
        s_new = s_ref[0, 0, hh] * gamma + kcol * vrow
        so_ref[0, 0, hh] = s_new
        o = jnp.sum(qcol * s_new, axis=0, keepdims=True)
        o_ref[pl.ds(b, 1), hh * dv:(hh + 1) * dv] = _ret_head_out(o, grow)


def _ret_decode(log_g, qkvg, state, *, heads, dk, dv, dec_block, hpg):
    n_layers, n_dec = state.shape[0], state.shape[1]
    n_hg = heads // hpg
    k_off = heads * dk // (hpg * dk)
    v_off = 2 * heads * dk // (hpg * dv)
    g_off = v_off + n_hg
    state_spec = pl.BlockSpec((1, 1, hpg, dk, dv), lambda hg, b: (0, b, hg, 0, 0))
    return pl.pallas_call(
        functools.partial(_ret_decode_kernel, hpg=hpg, dk=dk, dv=dv),
        grid=(n_hg, n_dec),
        in_specs=[
            pl.BlockSpec(memory_space=pltpu.SMEM),
            pl.BlockSpec((n_dec, hpg * dk), lambda hg, b: (dec_block, hg)),
            pl.BlockSpec((n_dec, hpg * dk), lambda hg, b: (dec_block, k_off + hg)),
            pl.BlockSpec((n_dec, hpg * dv), lambda hg, b: (dec_block, v_off + hg)),
            pl.BlockSpec((n_dec, hpg * dv), lambda hg, b: (dec_block, g_off + hg)),
            state_spec,
        ],
        out_specs=[pl.BlockSpec((n_dec, hpg * dv), lambda hg, b: (0, hg)), state_spec],
        out_shape=[jax.ShapeDtypeStruct((n_dec, heads * dv), F32),
                   jax.ShapeDtypeStruct(state.shape, F32)],
        scratch_shapes=[pltpu.VMEM((hpg * dk, n_dec), F32), pltpu.VMEM((hpg * dk, n_dec), F32)],
        compiler_params=_cparams(("arbitrary", "arbitrary"), 48),
        name="ret_decode",
    )(log_g, qkvg, qkvg, qkvg, qkvg, state)


def _conv_v_compute(x0, x1, x2, x3, cw_ref, cb_ref, coef_ref, xc_ref, v_ref):
    pre = cb_ref[...] + x3 * cw_ref[0:1, :] + x2 * cw_ref[1:2, :] + x1 * cw_ref[2:3, :] + x0 * cw_ref[3:4, :]
    xc_ref[...] = _silu(pre).astype(xc_ref.dtype)
    width = x0.shape[1]
    v = x0 * coef_ref[V_BLOCK - 1:V_BLOCK, :]
    for s in range(1, V_BLOCK):
        v = v + pltpu.roll(x0, s, 1) * coef_ref[V_BLOCK - 1 + s:V_BLOCK + s, :]
        v = v + pltpu.roll(x0, width - s, 1) * coef_ref[V_BLOCK - 1 - s:V_BLOCK - s, :]
    v_ref[...] = v.astype(v_ref.dtype)


def _conv_v_main_kernel(x_ref, prev_ref, cw_ref, cb_ref, coef_ref, xc_ref, v_ref, ext_ref):
    rows = x_ref.shape[0]
    x0 = x_ref[...]
    ext_ref[0:8, :] = prev_ref[...]
    ext_ref[8:8 + rows, :] = x0
    x1 = ext_ref[7:7 + rows, :]
    x2 = ext_ref[6:6 + rows, :]
    x3 = ext_ref[5:5 + rows, :]
    _conv_v_compute(x0, x1, x2, x3, cw_ref, cb_ref, coef_ref, xc_ref, v_ref)


def _conv_v_tail_kernel(x0_ref, x1_ref, x2_ref, x3_ref, cw_ref, cb_ref, coef_ref, _, __, xc_ref, v_ref):
    _conv_v_compute(x0_ref[...], x1_ref[...], x2_ref[...], x3_ref[...], cw_ref, cb_ref, coef_ref, xc_ref, v_ref)


def _v_coef(w_v):
    nb = w_v.shape[0]
    rows = []
    for s in range(-(V_BLOCK - 1), V_BLOCK):
        cols = []
        for d in range(V_BLOCK):
            c = d - s
            cols.append(w_v[:, c, d] if 0 <= c < V_BLOCK else jnp.zeros((nb,), w_v.dtype))
        rows.append(jnp.stack(cols, axis=1).reshape(nb * V_BLOCK))
    rows.append(jnp.zeros((nb * V_BLOCK,), w_v.dtype))
    return jnp.stack(rows, axis=0)


def _conv_v(xzo, x_tail, conv_w, conv_b, coef, *, n_tok, seq, meta0, inner, ct):
    m = xzo.shape[0]
    rb = CHUNK
    blocks_per_seq = seq // rb
    n_ct = inner // ct

    def prev_block(i):
        b = i // blocks_per_seq
        first = (meta0 + b * N_META + N_META - 8) // 8
        return jnp.where(i % blocks_per_seq == 0, first, i * (rb // 8) - 1)

    w_specs = [pl.BlockSpec((CONV_W, ct), lambda i, j: (0, j)),
               pl.BlockSpec((1, ct), lambda i, j: (0, j)),
               pl.BlockSpec((8, ct), lambda i, j: (0, j))]
    out_shape = [jax.ShapeDtypeStruct((m, inner), BF16), jax.ShapeDtypeStruct((m, inner), F32)]
    xc, v = pl.pallas_call(
        _conv_v_main_kernel,
        grid=(n_tok // rb, n_ct),
        in_specs=[pl.BlockSpec((rb, ct), lambda i, j: (i, j)),
                  pl.BlockSpec((8, ct), lambda i, j: (prev_block(i), j))] + w_specs,
        out_specs=[pl.BlockSpec((rb, ct), lambda i, j: (i, j))] * 2,
        out_shape=out_shape,
        scratch_shapes=[pltpu.VMEM((rb + 8, ct), F32)],
        compiler_params=_cparams(("parallel", "parallel")),
        name="conv_v_main",
    )(xzo, xzo, conv_w, conv_b, coef)

    n_tail = m - n_tok
    tb0 = n_tok // n_tail
    tail_in = pl.BlockSpec((n_tail, ct), lambda j: (0, j))
    tail_out = pl.BlockSpec((n_tail, ct), lambda j: (tb0, j))
    w_specs1 = [pl.BlockSpec((CONV_W, ct), lambda j: (0, j)),
                pl.BlockSpec((1, ct), lambda j: (0, j)),
                pl.BlockSpec((8, ct), lambda j: (0, j))]
    any_spec = pl.BlockSpec(memory_space=pl.ANY)
    xc, v = pl.pallas_call(
        _conv_v_tail_kernel,
        grid=(n_ct,),
        in_specs=[tail_out, tail_in, tail_in, tail_in] + w_specs1 + [any_spec, any_spec],
        out_specs=[tail_out, tail_out],
        out_shape=out_shape,
        input_output_aliases={7: 0, 8: 1},
        compiler_params=_cparams(("parallel",)),
        name="conv_v_tail",
    )(xzo, *x_tail, conv_w, conv_b, coef, xc, v)
    return xc, v


def _gates_kernel(q_ref, k_ref, v_ref, wq_ref, wk_ref, wv_ref, b_ref, g_ref, gt_ref, *, heads):
    acc = lax.dot_general(q_ref[...].astype(BF16), wq_ref[...].astype(BF16), NT_DIMS, preferred_element_type=F32)
    acc = acc + lax.dot_general(k_ref[...].astype(BF16), wk_ref[...].astype(BF16), NT_DIMS, preferred_element_type=F32)
    acc = acc + lax.dot_general(v_ref[...].astype(BF16), wv_ref[...].astype(BF16), NT_DIMS, preferred_element_type=F32)
    pre = acc + b_ref[...]
    col = lax.broadcasted_iota(jnp.int32, pre.shape, 1)
    log_sig = jnp.minimum(pre, 0.0) - jnp.log1p(jnp.exp(-jnp.abs(pre)))
    g = jnp.where(col >= heads, log_sig, pre)
    g_ref[...] = g
    gt_ref[...] = g.T[:gt_ref.shape[0], :]


def _gates(q, k, v, w_if, b_if, *, heads, bm):
    m, qk = q.shape
    inner = v.shape[1]
    lanes = 128
    w_t = jnp.zeros((lanes, w_if.shape[0]), F32).at[:2 * heads].set(w_if.T)
    bias = jnp.zeros((1, lanes), F32).at[0, :2 * heads].set(b_if)
    return pl.pallas_call(
        functools.partial(_gates_kernel, heads=heads),
        grid=(m // bm,),
        in_specs=[pl.BlockSpec((bm, qk), lambda i: (i, 0)),
                  pl.BlockSpec((bm, qk), lambda i: (i, 0)),
                  pl.BlockSpec((bm, inner), lambda i: (i, 0)),
                  pl.BlockSpec((lanes, qk), lambda i: (0, 0)),
                  pl.BlockSpec((lanes, qk), lambda i: (0, 1)),
                  pl.BlockSpec((lanes, inner), lambda i: (0, 2 * qk // inner)),
                  pl.BlockSpec((1, lanes), lambda i: (0, 0))],
        out_specs=[pl.BlockSpec((bm, lanes), lambda i: (i, 0)),
                   pl.BlockSpec((2 * heads, bm), lambda i: (0, i))],
        out_shape=[jax.ShapeDtypeStruct((m, lanes), F32), jax.ShapeDtypeStruct((2 * heads, m), F32)],
        compiler_params=_cparams(("parallel",), 48),
        name="gates",
    )(q, k, v, w_t, w_t, w_t, bias)


def _mlstm_head_out(h, o_pre, z, xc, gn, skip):
    h = _sigmoid(o_pre) * h
    mu = jnp.mean(h, axis=-1, keepdims=True)
    var = jnp.mean(jnp.square(h - mu), axis=-1, keepdims=True)
    hn = (h - mu) * lax.rsqrt(var + EPS) * gn
    return (hn + skip * xc) * _silu(z)


def _mlstm_chunk_kernel(q_ref, k_ref, v_ref, g_ref, gt_ref, op_ref, z_ref, xc_ref, gn_ref, sk_ref, *rest,
                        has_state, heads, dk):
    if has_state:
        c0_ref, n0_ref, m0_ref, _, o_ref, c_ref, n_ref, m_ref = rest
    else:
        _, o_ref, c_ref, n_ref, m_ref = rest
    hd = pl.program_id(1)
    ch = pl.program_id(2)

    @pl.when(ch == 0)
    def _():
        if has_state:
            c_ref[...] = c0_ref[...]
            n_ref[...] = n0_ref[...]
            m_ref[...] = m0_ref[...]
        else:
            c_ref[...] = jnp.zeros_like(c_ref)
            n_ref[...] = jnp.zeros_like(n_ref)
            m_ref[...] = jnp.zeros_like(m_ref)

    L = q_ref.shape[0]
    gates = g_ref[...]
    lane = lax.broadcasted_iota(jnp.int32, (1, gates.shape[1]), 1)
    logi_c = _select_lane(gates, lane == hd)
    logf_c = _select_lane(gates, lane == hd + heads)
    logi_r = gt_ref[pl.ds(hd, 1), :]
    logf_r = gt_ref[pl.ds(hd + heads, 1), :]

    ii = lax.broadcasted_iota(jnp.int32, (L, L), 0)
    jj = lax.broadcasted_iota(jnp.int32, (L, L), 1)
    causal = ii >= jj
    bcum_c = jnp.sum(jnp.where(causal, logf_r, 0.0), axis=1, keepdims=True)
    bcum_r = jnp.sum(jnp.where(ii <= jj, logf_c, 0.0), axis=0, keepdims=True)
    total = jnp.sum(logf_r, axis=1, keepdims=True)

    m_old = m_ref[0]
    dlog = jnp.where(causal, bcum_c - bcum_r + logi_r, -jnp.inf)
    inter = bcum_c + m_old
    m_row = jnp.maximum(inter, jnp.max(dlog, axis=1, keepdims=True))
    w = jnp.exp(dlog - m_row)
    s_inter = jnp.exp(inter - m_row)

    q = q_ref[...]
    ks = k_ref[...] * (dk ** -0.5)
    qb = q.astype(BF16)
    vb = v_ref[...].astype(BF16)
    c_old = c_ref[0, 0]
    n_old = n_ref[0]

    qk = lax.dot_general(qb, ks.astype(BF16), NT_DIMS, preferred_element_type=F32) * w
    num = jnp.dot(qk.astype(BF16), vb, preferred_element_type=F32)
    num = num + s_inter * jnp.dot(qb, c_old.astype(BF16), preferred_element_type=F32)
    den = jnp.sum(qk, axis=1, keepdims=True) + s_inter * jnp.sum(q * n_old, axis=1, keepdims=True)
    h = num / jnp.maximum(jnp.abs(den), jnp.exp(-m_row))

    m_new = m_row[L - 1:L, :]
    wk = jnp.exp(total - bcum_c + logi_c - m_new)
    s_c = jnp.exp(total + m_old - m_new)
    kw = ks * wk
    c_ref[0, 0] = s_c * c_old + lax.dot_general(kw.astype(BF16), vb, TN_DIMS, preferred_element_type=F32)
    n_ref[0] = s_c * n_old + jnp.sum(kw, axis=0, keepdims=True)
    m_ref[0] = m_new

    out = _mlstm_head_out(h, op_ref[...], z_ref[...], xc_ref[...].astype(F32), gn_ref[...], sk_ref[...])
    o_ref[...] = out.astype(o_ref.dtype)


def _mlstm_chunks(q, k, v, gates, gates_t, xzo, xc, gn, skip, x_out, state, *,
                  batch, heads, dk, dv, L, n_chunks, row0, name):
    rb0 = row0 // L
    inner = heads * dv
    z_off = inner // dv
    o_off = 2 * inner // dv

    def rows(b, h, c):
        return rb0 + b * n_chunks + c

    in_specs = [
        pl.BlockSpec((L, dk), lambda b, h, c: (rows(b, h, c), h)),
        pl.BlockSpec((L, dk), lambda b, h, c: (rows(b, h, c), h)),
        pl.BlockSpec((L, dv), lambda b, h, c: (rows(b, h, c), h)),
        pl.BlockSpec((L, gates.shape[1]), lambda b, h, c: (rows(b, h, c), 0)),
    ]
    args = [q, k, v, gates]
    if gates_t.ndim == 2:
        in_specs.append(pl.BlockSpec((2 * heads, L), lambda b, h, c: (0, rows(b, h, c))))
    else:
        in_specs.append(pl.BlockSpec((None, 2 * heads, L), lambda b, h, c: (b, 0, 0)))
    args.append(gates_t)
    in_specs += [
        pl.BlockSpec((L, dv), lambda b, h, c: (rows(b, h, c), o_off + h)),
        pl.BlockSpec((L, dv), lambda b, h, c: (rows(b, h, c), z_off + h)),
        pl.BlockSpec((L, dv), lambda b, h, c: (rows(b, h, c), h)),
        pl.BlockSpec((1, dv), lambda b, h, c: (0, h)),
        pl.BlockSpec((1, dv), lambda b, h, c: (0, h)),
    ]
    args += [xzo, xzo, xc, gn, skip]
    c_spec = pl.BlockSpec((1, 1, dk, dv), lambda b, h, c: (b, h, 0, 0))
    n_spec = pl.BlockSpec((1, 1, dk), lambda b, h, c: (b * heads + h, 0, 0))
    m_spec = pl.BlockSpec((1, 1, 1), lambda b, h, c: (b * heads + h, 0, 0))
    if state is not None:
        in_specs += [c_spec, n_spec, m_spec]
        args += list(state)
    in_specs.append(pl.BlockSpec(memory_space=pl.ANY))
    args.append(x_out)
    return pl.pallas_call(
        functools.partial(_mlstm_chunk_kernel, has_state=state is not None, heads=heads, dk=dk),
        grid=(batch, heads, n_chunks),
        in_specs=in_specs,
        out_specs=[pl.BlockSpec((L, dv), lambda b, h, c: (rows(b, h, c), h)), c_spec, n_spec, m_spec],
        out_shape=[jax.ShapeDtypeStruct(x_out.shape, x_out.dtype),
                   jax.ShapeDtypeStruct((batch, heads, dk, dv), F32),
                   jax.ShapeDtypeStruct((batch * heads, 1, dk), F32),
                   jax.ShapeDtypeStruct((batch * heads, 1, 1), F32)],
        input_output_aliases={len(args) - 1: 0},
        compiler_params=_cparams(("parallel", "parallel", "arbitrary"), 48),
        name=name,
    )(*args)


def _mlstm_decode_kernel(q_ref, k_ref, v_ref, g_ref, op_ref, z_ref, xc_ref, gn_ref, sk_ref,
                         c_ref, n_ref, m_ref, o_ref, co_ref, no_ref, mo_ref, qt_ref, kt_ref, xcf_ref,
                         *, hpg, heads, dk, dv):
    hg = pl.program_id(0)
    b = pl.program_id(1)

    @pl.when(b == 0)
    def _():
        qt_ref[...] = q_ref[...].T
        kt_ref[...] = k_ref[...].T
        xcf_ref[...] = xc_ref[...].astype(F32)

    n_dec = q_ref.shape[0]
    onehot = lax.broadcasted_iota(jnp.int32, (1, n_dec), 1) == b
    grow = g_ref[pl.ds(b, 1), :]
    lane = lax.broadcasted_iota(jnp.int32, grow.shape, 1)
    m_all = m_ref[0]
    mlane = lax.broadcasted_iota(jnp.int32, m_all.shape, 1)
    scale = dk ** -0.5
    m_out = jnp.zeros((1, hpg), F32)
    olane = lax.broadcasted_iota(jnp.int32, (1, hpg), 1)
    for hh in range(hpg):
        hd = hg * hpg + hh
        logi = _select_lane(grow, lane == hd)
        logf = _select_lane(grow, lane == hd + heads)
        m_old = _select_lane(m_all, mlane == hd)
        inter = logf + m_old
        m_new = jnp.maximum(inter, logi)
        w = jnp.exp(logi - m_new)
        s = jnp.exp(inter - m_new)
        kcol = _select_lane(kt_ref[hh * dk:(hh + 1) * dk, :], onehot) * scale
        qcol = _select_lane(qt_ref[hh * dk:(hh + 1) * dk, :], onehot)
        krow = k_ref[pl.ds(b, 1), hh * dk:(hh + 1) * dk] * scale
        qrow = q_ref[pl.ds(b, 1), hh * dk:(hh + 1) * dk]
        vrow = v_ref[pl.ds(b, 1), hh * dv:(hh + 1) * dv]
        c_new = s * c_ref[0, 0, hh] + (w * kcol) * vrow
        co_ref[0, 0, hh] = c_new
        n_new = s * n_ref[0, :, hh * dk:(hh + 1) * dk] + w * krow
        no_ref[0, :, hh * dk:(hh + 1) * dk] = n_new
        m_out = jnp.where(olane == hh, m_new, m_out)
        num = jnp.sum(qcol * c_new, axis=0, keepdims=True)
        den = jnp.sum(qrow * n_new, axis=1, keepdims=True)
        h = num / jnp.maximum(jnp.abs(den), jnp.exp(-m_new))
        cols = slice(hh * dv, (hh + 1) * dv)
        out = _mlstm_head_out(h, op_ref[pl.ds(b, 1), cols], z_ref[pl.ds(b, 1), cols],
                              xcf_ref[pl.ds(b, 1), cols], gn_ref[:, cols], sk_ref[:, cols])
        o_ref[pl.ds(b, 1), cols] = out
    mo_ref[0, 0] = m_out


def _mlstm_decode(q, k, v, gates, xzo, xc, gn, skip, c_state, n_state, m_state, *,
                  heads, dk, dv, dec_block, hpg):
    n_dec = c_state.shape[1]
    n_hg = heads // hpg
    inner = heads * dv
    z_off = inner // (hpg * dv)
    o_off = 2 * z_off
    n3 = n_state.reshape(n_dec, 1, heads * dk)
    m3 = m_state.reshape(n_dec, 1, heads)
    c_spec = pl.BlockSpec((1, 1, hpg, dk, dv), lambda hg, b: (0, b, hg, 0, 0))
    n_spec = pl.BlockSpec((1, 1, hpg * dk), lambda hg, b: (b, 0, hg))
    o, c_new, n_new, m_new = pl.pallas_call(
        functools.partial(_mlstm_decode_kernel, hpg=hpg, heads=heads, dk=dk, dv=dv),
        grid=(n_hg, n_dec),
        in_specs=[
            pl.BlockSpec((n_dec, hpg * dk), lambda hg, b: (dec_block, hg)),
            pl.BlockSpec((n_dec, hpg * dk), lambda hg, b: (dec_block, hg)),
            pl.BlockSpec((n_dec, hpg * dv), lambda hg, b: (dec_block, hg)),
            pl.BlockSpec((n_dec, gates.shape[1]), lambda hg, b: (dec_block, 0)),
            pl.BlockSpec((n_dec, hpg * dv), lambda hg, b: (dec_block, o_off + hg)),
            pl.BlockSpec((n_dec, hpg * dv), lambda hg, b: (dec_block, z_off + hg)),
            pl.BlockSpec((n_dec, hpg * dv), lambda hg, b: (dec_block, hg)),
            pl.BlockSpec((1, hpg * dv), lambda hg, b: (0, hg)),
            pl.BlockSpec((1, hpg * dv), lambda hg, b: (0, hg)),
            c_spec, n_spec,
            pl.BlockSpec((1, 1, heads), lambda hg, b: (b, 0, 0)),
        ],
        out_specs=[pl.BlockSpec((n_dec, hpg * dv), lambda hg, b: (0, hg)), c_spec, n_spec,
                   pl.BlockSpec((1, 1, 1, hpg), lambda hg, b: (hg, b, 0, 0))],
        out_shape=[jax.ShapeDtypeStruct((n_dec, inner), F32),
                   jax.ShapeDtypeStruct(c_state.shape, F32),
                   jax.ShapeDtypeStruct(n3.shape, F32),
                   jax.ShapeDtypeStruct((n_hg, n_dec, 1, hpg), F32)],
        scratch_shapes=[pltpu.VMEM((hpg * dk, n_dec), F32), pltpu.VMEM((hpg * dk, n_dec), F32),
                        pltpu.VMEM((n_dec, hpg * dv), F32)],
        compiler_params=_cparams(("arbitrary", "arbitrary"), 48),
        name="mlstm_decode",
    )(q, k, v, gates, xzo, xzo, xc, gn, skip, c_state, n3, m3)
    m_new = jnp.transpose(m_new[:, :, 0, :], (1, 0, 2)).reshape(n_dec, heads)
    return o, c_new, n_new.reshape(n_state.shape), m_new.reshape(m_state.shape)


def kernel(x_prompt, x_sample, state_ret_S, state_mlstm_C, state_mlstm_n, state_mlstm_m, state_mlstm_conv,
           meta_tokens, norm_g, final_norm_g, ret_w_in, ret_w_out,
           ml_w_in, ml_conv_w, ml_conv_b, ml_w_q, ml_w_k, ml_w_v, ml_w_if, ml_b_if, ml_skip, ml_gn_g, ml_w_out):
    batch, seq, d = x_prompt.shape
    n_dec = x_sample.shape[0]
    n_tok = batch * seq
    n_chunks = seq // CHUNK
    dec0 = n_tok
    meta0 = n_tok + n_dec
    n_meta = batch * N_META
    m = -(-(meta0 + n_meta) // ROW_ALIGN) * ROW_ALIGN
    assert seq % CHUNK == 0 and n_dec == CHUNK and x_sample.shape[1] == 1
    assert state_ret_S.shape[0] == 1 and state_mlstm_C.shape[0] == 1
    dec_block = dec0 // n_dec

    ret_dk = d // RET_HEADS
    ret_dv = 2 * d // RET_HEADS
    ret_vw = RET_HEADS * ret_dv
    inner = 2 * d
    m_dk = d // M_HEADS
    m_dv = inner // M_HEADS

    bm = _pick(m, (768, 512, 384, 256))
    bm_small = _pick(m, (384, 256))

    h0 = jnp.concatenate([
        x_prompt.reshape(n_tok, d),
        x_sample.reshape(n_dec, d),
        jnp.broadcast_to(meta_tokens[None].astype(F32), (batch, N_META, d)).reshape(n_meta, d),
        jnp.zeros((m - meta0 - n_meta, d), F32)], axis=0)

    half = ret_dk // 2
    inv = ROPE_BASE ** (-jnp.arange(half, dtype=F32) / half)
    pos = jnp.concatenate([
        jnp.tile(N_META + jnp.arange(seq, dtype=jnp.int32), batch),
        jnp.full((n_dec,), PAST_LEN, jnp.int32),
        jnp.tile(jnp.arange(N_META, dtype=jnp.int32), batch),
        jnp.zeros((m - meta0 - n_meta,), jnp.int32)])
    ang = pos.astype(F32)[:, None] * inv[None, :]
    cos, sin = jnp.cos(ang), jnp.sin(ang)

    x0 = _rmsnorm(h0, norm_g[0], out_dtype=BF16, block_rows=ROW_ALIGN)
    n_rope = RET_HEADS * ret_dk // 512
    qkvg = _matmul(x0, ret_w_in[0], bm=bm, bn=512, cos=cos, sin=sin,
                   rope=(n_rope, n_rope, ret_dk, ret_dk ** -0.5), name="ret_in_proj")
    log_g = jnp.log1p(-jnp.exp2(-5.0 - jnp.arange(RET_HEADS, dtype=F32)))
    xr = jnp.zeros((m, ret_vw), BF16)
    xr, s_meta = _ret_chunks(log_g, qkvg, xr, None, batch=batch, heads=RET_HEADS, dk=ret_dk, dv=ret_dv,
                             L=N_META, n_chunks=1, row0=meta0, name="ret_meta")
    xr, s_prompt = _ret_chunks(log_g, qkvg, xr, s_meta, batch=batch, heads=RET_HEADS, dk=ret_dk, dv=ret_dv,
                               L=CHUNK, n_chunks=n_chunks, row0=0, name="ret_chunks")
    o_dec, s_sample = _ret_decode(log_g, qkvg, state_ret_S, heads=RET_HEADS, dk=ret_dk, dv=ret_dv,
                                  dec_block=dec_block, hpg=min(8, RET_HEADS))
    xr = lax.dynamic_update_slice(xr, o_dec.astype(BF16), (dec0, 0))
    h1 = _matmul(xr, ret_w_out[0], bm=bm_small, bn=512, res=h0, w_single_buffer=True, vmem_mib=56,
                 name="ret_out_proj")

    x1 = _rmsnorm(h1, norm_g[1], out_dtype=BF16, block_rows=ROW_ALIGN)
    xzo = _matmul(x1, ml_w_in[0], bm=bm, bn=512, name="ml_in_proj")

    xm_meta = xzo[meta0:meta0 + n_meta, :inner].reshape(batch, N_META, inner)
    conv_state = state_mlstm_conv[0]
    pad_rows = jnp.zeros((m - meta0 - n_meta, inner), F32)

    def tail_prev(kk):
        meta_prev = jnp.pad(xm_meta, ((0, 0), (kk, 0), (0, 0)))[:, :N_META].reshape(n_meta, inner)
        return jnp.concatenate([conv_state[:, CONV_W - 1 - kk], meta_prev, pad_rows], axis=0)

    x_tail = tuple(tail_prev(kk) for kk in range(1, CONV_W))
    xc, v = _conv_v(xzo, x_tail, ml_conv_w[0], ml_conv_b[0].reshape(1, inner), _v_coef(ml_w_v[0]),
                    n_tok=n_tok, seq=seq, meta0=meta0, inner=inner, ct=min(inner, 2048))

    q = _matmul(xc, ml_w_q[0], bm=bm_small, bn=512, w_single_buffer=True, vmem_mib=56, name="ml_q_proj")
    k = _matmul(xc, ml_w_k[0], bm=bm_small, bn=512, w_single_buffer=True, vmem_mib=56, name="ml_k_proj")
    gates, gates_t = _gates(q, k, v, ml_w_if[0], ml_b_if[0], heads=M_HEADS, bm=128)

    gn = ml_gn_g[0].reshape(1, inner)
    skip = ml_skip[0].reshape(1, inner)
    xm_out = jnp.zeros((m, inner), BF16)
    gt_meta = gates_t[:, meta0:meta0 + n_meta].reshape(2 * M_HEADS, batch, N_META).transpose(1, 0, 2)
    xm_out, c_meta, n_meta_s, m_meta = _mlstm_chunks(
        q, k, v, gates, gt_meta, xzo, xc, gn, skip, xm_out, None,
        batch=batch, heads=M_HEADS, dk=m_dk, dv=m_dv, L=N_META, n_chunks=1, row0=meta0, name="mlstm_meta")
    xm_out, c_prompt, n_prompt, m_prompt = _mlstm_chunks(
        q, k, v, gates, gates_t, xzo, xc, gn, skip, xm_out, (c_meta, n_meta_s, m_meta),
        batch=batch, heads=M_HEADS, dk=m_dk, dv=m_dv, L=CHUNK, n_chunks=n_chunks, row0=0, name="mlstm_chunks")
    h_dec, c_sample, n_sample, m_sample = _mlstm_decode(
        q, k, v, gates, xzo, xc, gn, skip, state_mlstm_C, state_mlstm_n, state_mlstm_m,
        heads=M_HEADS, dk=m_dk, dv=m_dv, dec_block=dec_block, hpg=min(2, M_HEADS))
    xm_out = lax.dynamic_update_slice(xm_out, h_dec.astype(BF16), (dec0, 0))
    h2 = _matmul(xm_out, ml_w_out[0], bm=bm_small, bn=512, res=h1, w_single_buffer=True, vmem_mib=56,
                 name="ml_out_proj")

    y_prompt = _rmsnorm(h2, final_norm_g, out_dtype=F32, block_rows=CHUNK, n_blocks=n_tok // CHUNK)
    y_sample = _rmsnorm(h2, final_norm_g, out_dtype=F32, block_rows=CHUNK, first_block=dec_block, n_blocks=1)
    conv_prompt = xzo[:n_tok, :inner].reshape(batch, seq, inner)[:, seq - (CONV_W - 1):]
    conv_sample = jnp.concatenate([conv_state[:, 1:], xzo[dec0:dec0 + n_dec, None, :inner]], axis=1)
    return (y_prompt.reshape(batch, seq, d), y_sample.reshape(n_dec, 1, d),
            s_prompt[None], s_sample,
            c_prompt[None], c_sample,
            n_prompt.reshape(1, batch, M_HEADS, m_dk), n_sample,
            m_prompt.reshape(1, batch, M_HEADS), m_sample,
            conv_prompt[None], conv_sample[None])
```

```python
import functools

import jax
import jax.numpy as jnp
from jax import lax
from jax.experimental import pallas as pl
from jax.experimental.pallas import tpu as pltpu

F32 = jnp.float32
BF16 = jnp.bfloat16

N_META = 16
CHUNK = 128
EPS = 1e-6
RET_HEADS = 16
M_HEADS = 8
CONV_W = 4
V_BLOCK = 4
ROPE_BASE = 10000.0
PAST_LEN = 16384

ROW_ALIGN = 256
MIB = 1024 * 1024
NT_DIMS = (((1,), (1,)), ((), ()))
TN_DIMS = (((0,), (0,)), ((), ()))


def _cparams(sem, vmem_mib=None):
    kw = dict(dimension_semantics=sem)
    if vmem_mib is not None:
        kw["vmem_limit_bytes"] = int(vmem_mib * MIB)
    return pltpu.CompilerParams(**kw)


def _pick(n, candidates):
    for c in candidates:
        if n % c == 0:
            return c
    raise ValueError(f"no tile for {n} in {candidates}")


def _sigmoid(x):
    return 1.0 / (1.0 + jnp.exp(-x))


def _silu(x):
    return x * _sigmoid(x)


def _rmsnorm_kernel(x_ref, g_ref, o_ref):
    x = x_ref[...]
    y = x * lax.rsqrt(jnp.mean(x * x, axis=-1, keepdims=True) + EPS)
    o_ref[...] = (y * g_ref[...]).astype(o_ref.dtype)


def _rmsnorm(x, g, *, out_dtype, block_rows, first_block=0, n_blocks=None):
    m, d = x.shape
    if n_blocks is None:
        n_blocks = m // block_rows
    return pl.pallas_call(
        _rmsnorm_kernel,
        grid=(n_blocks,),
        in_specs=[pl.BlockSpec((block_rows, d), lambda i: (i + first_block, 0)),
                  pl.BlockSpec((1, d), lambda i: (0, 0))],
        out_specs=pl.BlockSpec((block_rows, d), lambda i: (i, 0)),
        out_shape=jax.ShapeDtypeStruct((n_blocks * block_rows, d), out_dtype),
        compiler_params=_cparams(("parallel",)),
        name="rmsnorm",
    )(x, g.reshape(1, d))


def _matmul_kernel(x_ref, w_ref, *rest, rope, act, has_res, n_sub):
    pos = 0
    if rope is not None:
        cos_ref, sin_ref = rest[0], rest[1]
        pos = 2
    if has_res:
        res_ref = rest[pos]
        pos += 1
    o_ref, wb_ref = rest[pos], rest[pos + 1]
    j = pl.program_id(0)

    @pl.when(pl.program_id(1) == 0)
    def _():
        wb_ref[...] = w_ref[...].astype(BF16)

    sub = x_ref.shape[0] // n_sub
    for r in range(n_sub):
        rows = slice(r * sub, (r + 1) * sub)
        acc = jnp.dot(x_ref[rows, :], wb_ref[...], preferred_element_type=F32)
        if has_res:
            acc = res_ref[rows, :] + acc
        if act is not None:
            n_first, first, rest_kind = act
            sig = 0.5 * jnp.tanh(0.5 * acc) + 0.5
            vals = {"id": acc, "silu": acc * sig, "sigmoid": sig}
            acc = jnp.where(j < n_first, vals[first], vals[rest_kind])
        if rope is None:
            o_ref[rows, :] = acc.astype(o_ref.dtype)
            continue
        n_q, head_dim, k_scale = rope
        half = head_dim // 2
        scale = jnp.where(j < n_q, 1.0, k_scale).astype(F32)
        cos = cos_ref[rows, :]
        sin = sin_ref[rows, :]
        for h in range(o_ref.shape[1] // head_dim):
            lo = h * head_dim
            x1 = acc[:, lo:lo + half]
            x2 = acc[:, lo + half:lo + head_dim]
            o_ref[rows, lo:lo + half] = ((x1 * cos - x2 * sin) * scale).astype(o_ref.dtype)
            o_ref[rows, lo + half:lo + head_dim] = ((x1 * sin + x2 * cos) * scale).astype(o_ref.dtype)


def _matmul(x, w, *, bm, bn, col0=0, n_cols=None, out_dtype=F32, res=None, rope=None, cos=None, sin=None,
            act=None, n_sub=1, w_single_buffer=False, vmem_mib=48, name="matmul"):
    m, k = x.shape
    n = w.shape[1] - col0 if n_cols is None else n_cols
    j0 = col0 // bn
    grid = (n // bn, m // bm)
    w_kw = dict(pipeline_mode=pl.Buffered(1)) if w_single_buffer else {}
    in_specs = [pl.BlockSpec((bm, k), lambda j, i: (i, 0)),
                pl.BlockSpec((k, bn), lambda j, i: (0, j + j0), **w_kw)]
    args = [x, w]
    if rope is not None:
        half = rope[1] // 2
        in_specs += [pl.BlockSpec((bm, half), lambda j, i: (i, 0)),
                     pl.BlockSpec((bm, half), lambda j, i: (i, 0))]
        args += [cos, sin]
    if res is not None:
        in_specs.append(pl.BlockSpec((bm, bn), lambda j, i: (i, j)))
        args.append(res)
    return pl.pallas_call(
        functools.partial(_matmul_kernel, rope=rope, act=act, has_res=res is not None, n_sub=n_sub),
        grid=grid,
        in_specs=in_specs,
        out_specs=pl.BlockSpec((bm, bn), lambda j, i: (i, j)),
        out_shape=jax.ShapeDtypeStruct((m, n), out_dtype),
        scratch_shapes=[pltpu.VMEM((k, bn), BF16)],
        compiler_params=_cparams(("arbitrary", "arbitrary"), vmem_mib),
        name=name,
    )(*args)


def _ret_head_out(o, gate):
    o = o * lax.rsqrt(jnp.mean(o * o, axis=-1, keepdims=True) + EPS)
    return gate * o


def _ret_chunk_kernel(lg_ref, q_ref, k_ref, v_ref, g_ref, *rest, has_s0, hpg, dk, dv):
    if has_s0:
        s0_ref, _, o_ref, s_ref = rest
    else:
        _, o_ref, s_ref = rest
    hg = pl.program_id(1)

    @pl.when(pl.program_id(2) == 0)
    def _():
        if has_s0:
            s_ref[...] = s0_ref[...]
        else:
            s_ref[...] = jnp.zeros_like(s_ref)

    L = q_ref.shape[0]
    ii = lax.broadcasted_iota(jnp.int32, (L, L), 0)
    jj = lax.broadcasted_iota(jnp.int32, (L, L), 1)
    causal = ii >= jj
    diff = jnp.where(causal, (ii - jj).astype(F32), 0.0)
    idx = lax.broadcasted_iota(jnp.int32, (L, 1), 0).astype(F32)
    for hh in range(hpg):
        lg = jnp.full((1, 1), lg_ref[hg * hpg + hh], F32)
        q = q_ref[:, hh * dk:(hh + 1) * dk].astype(BF16)
        k = k_ref[:, hh * dk:(hh + 1) * dk]
        v = v_ref[:, hh * dv:(hh + 1) * dv].astype(BF16)
        s = s_ref[0, hh]
        decay = jnp.where(causal, jnp.exp(lg * diff), 0.0)
        scores = lax.dot_general(q, k.astype(BF16), NT_DIMS, preferred_element_type=F32) * decay
        inner = jnp.dot(scores.astype(BF16), v, preferred_element_type=F32)
        q_dec = jnp.exp(lg * (idx + 1.0))
        cross = jnp.dot(q, s.astype(BF16), preferred_element_type=F32) * q_dec
        k_dec = jnp.exp(lg * (L - 1.0 - idx))
        kd = (k * k_dec).astype(BF16)
        s_ref[0, hh] = s * jnp.exp(lg * float(L)) + lax.dot_general(kd, v, TN_DIMS, preferred_element_type=F32)
        out = _ret_head_out(inner + cross, g_ref[:, hh * dv:(hh + 1) * dv])
        o_ref[:, hh * dv:(hh + 1) * dv] = out.astype(o_ref.dtype)


def _ret_chunks(log_g, qk, vg, x_out, s0, *, batch, heads, dk, dv, L, n_chunks, row0, hpg, name):
    rb0 = row0 // L
    n_hg = heads // hpg

    def rows(b, c):
        return rb0 + b * n_chunks + c

    s_spec = pl.BlockSpec((1, hpg, dk, dv), lambda b, h, c: (b, h, 0, 0))
    in_specs = [
        pl.BlockSpec(memory_space=pltpu.SMEM),
        pl.BlockSpec((L, hpg * dk), lambda b, h, c: (rows(b, c), h)),
        pl.BlockSpec((L, hpg * dk), lambda b, h, c: (rows(b, c), n_hg + h)),
        pl.BlockSpec((L, hpg * dv), lambda b, h, c: (rows(b, c), h)),
        pl.BlockSpec((L, hpg * dv), lambda b, h, c: (rows(b, c), n_hg + h)),
    ]
    args = [log_g, qk, qk, vg, vg]
    if s0 is not None:
        in_specs.append(s_spec)
        args.append(s0)
    in_specs.append(pl.BlockSpec(memory_space=pl.ANY))
    args.append(x_out)
    return pl.pallas_call(
        functools.partial(_ret_chunk_kernel, has_s0=s0 is not None, hpg=hpg, dk=dk, dv=dv),
        grid=(batch, n_hg, n_chunks),
        in_specs=in_specs,
        out_specs=[pl.BlockSpec((L, hpg * dv), lambda b, h, c: (rows(b, c), h)), s_spec],
        out_shape=[jax.ShapeDtypeStruct(x_out.shape, x_out.dtype),
                   jax.ShapeDtypeStruct((batch, heads, dk, dv), F32)],
        input_output_aliases={len(args) - 1: 0},
        compiler_params=_cparams(("parallel", "parallel", "arbitrary"), 48),
        name=name,
    )(*args)


def _select_lane(x, lane_onehot):
    return jnp.sum(jnp.where(lane_onehot, x, 0.0), axis=1, keepdims=True)


def _ret_decode_kernel(lg_ref, q_ref, k_ref, v_ref, g_ref, s_ref, o_ref, so_ref, qt_ref, kt_ref,
                       *, hpg, dk, dv):
    hg = pl.program_id(0)
    b = pl.program_id(1)

    @pl.when(b == 0)
    def _():
        qt_ref[...] = q_ref[...].T
        kt_ref[...] = k_ref[...].T

    n_dec = q_ref.shape[0]
    onehot = lax.broadcasted_iota(jnp.int32, (1, n_dec), 1) == b
    for hh in range(hpg):
        gamma = jnp.exp(jnp.full((1, 1), lg_ref[hg * hpg + hh], F32))
        kcol = _select_lane(kt_ref[hh * dk:(hh + 1) * dk, :], onehot)
        qcol = _select_lane(qt_ref[hh * dk:(hh + 1) * dk, :], onehot)
        vrow = v_ref[pl.ds(b, 1), hh * dv:(hh + 1) * dv]
        grow = g_ref[pl.ds(b, 1), hh * dv:(hh + 1) * dv]
        s_new = s_ref[0, 0, hh] * gamma + kcol * vrow
        so_ref[0, 0, hh] = s_new
        o = jnp.sum(qcol * s_new, axis=0, keepdims=True)
        o_ref[pl.ds(b, 1), hh * dv:(hh + 1) * dv] = _ret_head_out(o, grow)


def _ret_decode(log_g, qk, vg, state, *, heads, dk, dv, dec_block, hpg):
    n_dec = state.shape[1]
    n_hg = heads // hpg
    state_spec = pl.BlockSpec((1, 1, hpg, dk, dv), lambda hg, b: (0, b, hg, 0, 0))
    return pl.pallas_call(
        functools.partial(_ret_decode_kernel, hpg=hpg, dk=dk, dv=dv),
        grid=(n_hg, n_dec),
        in_specs=[
            pl.BlockSpec(memory_space=pltpu.SMEM),
            pl.BlockSpec((n_dec, hpg * dk), lambda hg, b: (dec_block, hg)),
            pl.BlockSpec((n_dec, hpg * dk), lambda hg, b: (dec_block, n_hg + hg)),
            pl.BlockSpec((n_dec, hpg * dv), lambda hg, b: (dec_block, hg)),
            pl.BlockSpec((n_dec, hpg * dv), lambda hg, b: (dec_block, n_hg + hg)),
            state_spec,
        ],
        out_specs=[pl.BlockSpec((n_dec, hpg * dv), lambda hg, b: (0, hg)), state_spec],
        out_shape=[jax.ShapeDtypeStruct((n_dec, heads * dv), F32),
                   jax.ShapeDtypeStruct(state.shape, F32)],
        scratch_shapes=[pltpu.VMEM((hpg * dk, n_dec), F32), pltpu.VMEM((hpg * dk, n_dec), F32)],
        compiler_params=_cparams(("arbitrary", "arbitrary"), 48),
        name="ret_decode",
    )(log_g, qk, qk, vg, vg, state)


LANES = 128


def _conv_v_compute(x0, x1, x2, x3, cw_ref, cb_ref, whi_ref, wlo_ref, xc_ref, v_ref):
    pre = cb_ref[...] + x3 * cw_ref[0:1, :] + x2 * cw_ref[1:2, :] + x1 * cw_ref[2:3, :] + x0 * cw_ref[3:4, :]
    xc_ref[...] = _silu(pre).astype(xc_ref.dtype)
    xh = x0.astype(BF16)
    xl = (x0 - xh.astype(F32)).astype(BF16)
    for c in range(x0.shape[1] // LANES):
        cols = slice(c * LANES, (c + 1) * LANES)
        whi = whi_ref[c]
        v = jnp.dot(xh[:, cols], whi, preferred_element_type=F32)
        v = v + jnp.dot(xl[:, cols], whi, preferred_element_type=F32)
        v = v + jnp.dot(xh[:, cols], wlo_ref[c], preferred_element_type=F32)
        v_ref[:, cols] = v.astype(v_ref.dtype)


def _conv_v_main_kernel(x_ref, prev_ref, cw_ref, cb_ref, whi_ref, wlo_ref, xc_ref, v_ref, ext_ref):
    rows = x_ref.shape[0]
    x0 = x_ref[...]
    ext_ref[0:8, :] = prev_ref[...]
    ext_ref[8:8 + rows, :] = x0
    x1 = ext_ref[7:7 + rows, :]
    x2 = ext_ref[6:6 + rows, :]
    x3 = ext_ref[5:5 + rows, :]
    _conv_v_compute(x0, x1, x2, x3, cw_ref, cb_ref, whi_ref, wlo_ref, xc_ref, v_ref)


def _conv_v_tail_kernel(x0_ref, x1_ref, x2_ref, x3_ref, cw_ref, cb_ref, whi_ref, wlo_ref, _, __, xc_ref, v_ref):
    _conv_v_compute(x0_ref[...], x1_ref[...], x2_ref[...], x3_ref[...], cw_ref, cb_ref, whi_ref, wlo_ref,
                    xc_ref, v_ref)


def _v_blockdiag(w_v):
    g = LANES // V_BLOCK
    w = w_v.reshape(w_v.shape[0] // g, g, V_BLOCK, V_BLOCK)
    bd = jnp.einsum('bgcd,gh->bgchd', w, jnp.eye(g, dtype=w_v.dtype)).reshape(-1, LANES, LANES)
    hi = bd.astype(BF16)
    lo = (bd - hi.astype(F32)).astype(BF16)
    return hi, lo


def _conv_v(xm, x_tail, conv_w, conv_b, w_bd, *, n_tok, seq, meta0, inner, ct):
    m = xm.shape[0]
    rb = CHUNK
    blocks_per_seq = seq // rb
    n_ct = inner // ct

    def prev_block(i):
        b = i // blocks_per_seq
        first = (meta0 + b * N_META + N_META - 8) // 8
        return jnp.where(i % blocks_per_seq == 0, first, i * (rb // 8) - 1)

    nb = ct // LANES
    w_specs = [pl.BlockSpec((CONV_W, ct), lambda j, i: (0, j)),
               pl.BlockSpec((1, ct), lambda j, i: (0, j)),
               pl.BlockSpec((nb, LANES, LANES), lambda j, i: (j, 0, 0)),
               pl.BlockSpec((nb, LANES, LANES), lambda j, i: (j, 0, 0))]
    out_shape = [jax.ShapeDtypeStruct((m, inner), BF16), jax.ShapeDtypeStruct((m, inner), F32)]
    xc, v = pl.pallas_call(
        _conv_v_main_kernel,
        grid=(n_ct, n_tok // rb),
        in_specs=[pl.BlockSpec((rb, ct), lambda j, i: (i, j)),
                  pl.BlockSpec((8, ct), lambda j, i: (prev_block(i), j))] + w_specs,
        out_specs=[pl.BlockSpec((rb, ct), lambda j, i: (i, j))] * 2,
        out_shape=out_shape,
        scratch_shapes=[pltpu.VMEM((rb + 8, ct), F32)],
        compiler_params=_cparams(("parallel", "parallel")),
        name="conv_v_main",
    )(xm, xm, conv_w, conv_b, *w_bd)

    n_tail = m - n_tok
    tb0 = n_tok // n_tail
    tail_in = pl.BlockSpec((n_tail, ct), lambda j: (0, j))
    tail_out = pl.BlockSpec((n_tail, ct), lambda j: (tb0, j))
    w_specs1 = [pl.BlockSpec((CONV_W, ct), lambda j: (0, j)),
                pl.BlockSpec((1, ct), lambda j: (0, j)),
                pl.BlockSpec((nb, LANES, LANES), lambda j: (j, 0, 0)),
                pl.BlockSpec((nb, LANES, LANES), lambda j: (j, 0, 0))]
    any_spec = pl.BlockSpec(memory_space=pl.ANY)
    xc, v = pl.pallas_call(
        _conv_v_tail_kernel,
        grid=(n_ct,),
        in_specs=[tail_out, tail_in, tail_in, tail_in] + w_specs1 + [any_spec, any_spec],
        out_specs=[tail_out, tail_out],
        out_shape=out_shape,
        input_output_aliases={8: 0, 9: 1},
        compiler_params=_cparams(("parallel",)),
        name="conv_v_tail",
    )(xm, *x_tail, conv_w, conv_b, *w_bd, xc, v)
    return xc, v


def _gates_kernel(q_ref, k_ref, v_ref, wq_ref, wk_ref, wv_ref, b_ref, g_ref, gt_ref, *, heads):
    acc = lax.dot_general(q_ref[...].astype(BF16), wq_ref[...].astype(BF16), NT_DIMS, preferred_element_type=F32)
    acc = acc + lax.dot_general(k_ref[...].astype(BF16), wk_ref[...].astype(BF16), NT_DIMS, preferred_element_type=F32)
    acc = acc + lax.dot_general(v_ref[...].astype(BF16), wv_ref[...].astype(BF16), NT_DIMS, preferred_element_type=F32)
    pre = acc + b_ref[...]
    col = lax.broadcasted_iota(jnp.int32, pre.shape, 1)
    log_sig = jnp.minimum(pre, 0.0) - jnp.log1p(jnp.exp(-jnp.abs(pre)))
    g = jnp.where(col >= heads, log_sig, pre)
    g_ref[...] = g
    gt_ref[...] = g.T[:gt_ref.shape[0], :]


def _gates(q, k, v, w_if, b_if, *, heads, bm):
    m, qk = q.shape
    inner = v.shape[1]
    lanes = 128
    w_t = jnp.zeros((lanes, w_if.shape[0]), F32).at[:2 * heads].set(w_if.T)
    bias = jnp.zeros((1, lanes), F32).at[0, :2 * heads].set(b_if)
    return pl.pallas_call(
        functools.partial(_gates_kernel, heads=heads),
        grid=(m // bm,),
        in_specs=[pl.BlockSpec((bm, qk), lambda i: (i, 0)),
                  pl.BlockSpec((bm, qk), lambda i: (i, 0)),
                  pl.BlockSpec((bm, inner), lambda i: (i, 0)),
                  pl.BlockSpec((lanes, qk), lambda i: (0, 0)),
                  pl.BlockSpec((lanes, qk), lambda i: (0, 1)),
                  pl.BlockSpec((lanes, inner), lambda i: (0, 2 * qk // inner)),
                  pl.BlockSpec((1, lanes), lambda i: (0, 0))],
        out_specs=[pl.BlockSpec((bm, lanes), lambda i: (i, 0)),
                   pl.BlockSpec((2 * heads, bm), lambda i: (0, i))],
        out_shape=[jax.ShapeDtypeStruct((m, lanes), F32), jax.ShapeDtypeStruct((2 * heads, m), F32)],
        compiler_params=_cparams(("parallel",), 48),
        name="gates",
    )(q, k, v, w_t, w_t, w_t, bias)


def _mlstm_head_out(h, o_gate, z_gate, xc, gn, skip):
    h = o_gate * h
    mu = jnp.mean(h, axis=-1, keepdims=True)
    var = jnp.mean(jnp.square(h - mu), axis=-1, keepdims=True)
    hn = (h - mu) * lax.rsqrt(var + EPS) * gn
    return (hn + skip * xc) * z_gate


def _mlstm_chunk_kernel(q_ref, k_ref, v_ref, g_ref, gt_ref, op_ref, z_ref, xc_ref, gn_ref, sk_ref, *rest,
                        has_state, heads, hpg, dk, dv):
    if has_state:
        c0_ref, n0_ref, m0_ref, _, o_ref, c_ref, n_ref, m_ref = rest
    else:
        _, o_ref, c_ref, n_ref, m_ref = rest
    hg = pl.program_id(1)

    @pl.when(pl.program_id(2) == 0)
    def _():
        if has_state:
            c_ref[...] = c0_ref[...]
            n_ref[...] = n0_ref[...]
            m_ref[...] = m0_ref[...]
        else:
            c_ref[...] = jnp.zeros_like(c_ref)
            n_ref[...] = jnp.zeros_like(n_ref)
            m_ref[...] = jnp.zeros_like(m_ref)

    L = q_ref.shape[0]
    gates = g_ref[...]
    lane = lax.broadcasted_iota(jnp.int32, (1, gates.shape[1]), 1)
    ii = lax.broadcasted_iota(jnp.int32, (L, L), 0)
    jj = lax.broadcasted_iota(jnp.int32, (L, L), 1)
    causal = ii >= jj
    for hh in range(hpg):
        hd = hg * hpg + hh
        logi_c = _select_lane(gates, lane == hd)
        logf_c = _select_lane(gates, lane == hd + heads)
        logi_r = gt_ref[pl.ds(hd, 1), :]
        logf_r = gt_ref[pl.ds(hd + heads, 1), :]
        bcum_c = jnp.sum(jnp.where(causal, logf_r, 0.0), axis=1, keepdims=True)
        bcum_r = jnp.sum(jnp.where(ii <= jj, logf_c, 0.0), axis=0, keepdims=True)
        total = jnp.sum(logf_r, axis=1, keepdims=True)

        m_old = m_ref[hh]
        dlog = jnp.where(causal, bcum_c - bcum_r + logi_r, -jnp.inf)
        inter = bcum_c + m_old
        m_row = jnp.maximum(inter, jnp.max(dlog, axis=1, keepdims=True))
        w = jnp.exp(dlog - m_row)
        s_inter = jnp.exp(inter - m_row)

        q = q_ref[:, hh * dk:(hh + 1) * dk]
        ks = k_ref[:, hh * dk:(hh + 1) * dk] * (dk ** -0.5)
        qb = q.astype(BF16)
        cols = slice(hh * dv, (hh + 1) * dv)
        vb = v_ref[:, cols].astype(BF16)
        c_old = c_ref[0, hh]
        n_old = n_ref[hh]

        qk = lax.dot_general(qb, ks.astype(BF16), NT_DIMS, preferred_element_type=F32) * w
        num = jnp.dot(qk.astype(BF16), vb, preferred_element_type=F32)
        num = num + s_inter * jnp.dot(qb, c_old.astype(BF16), preferred_element_type=F32)
        den = jnp.sum(qk, axis=1, keepdims=True) + s_inter * jnp.sum(q * n_old, axis=1, keepdims=True)
        h = num / jnp.maximum(jnp.abs(den), jnp.exp(-m_row))

        m_new = m_row[L - 1:L, :]
        wk = jnp.exp(total - bcum_c + logi_c - m_new)
        s_c = jnp.exp(total + m_old - m_new)
        kw = ks * wk
        c_ref[0, hh] = s_c * c_old + lax.dot_general(kw.astype(BF16), vb, TN_DIMS, preferred_element_type=F32)
        n_ref[hh] = s_c * n_old + jnp.sum(kw, axis=0, keepdims=True)
        m_ref[hh] = m_new

        out = _mlstm_head_out(h, op_ref[:, cols], z_ref[:, cols], xc_ref[:, cols].astype(F32),
                              gn_ref[:, cols], sk_ref[:, cols])
        o_ref[:, cols] = out.astype(o_ref.dtype)


def _mlstm_chunks(q, k, v, gates, gates_t, zo, xc, gn, skip, x_out, state, *,
                  batch, heads, dk, dv, L, n_chunks, row0, hpg, name):
    rb0 = row0 // L
    n_hg = heads // hpg

    def rows(b, c):
        return rb0 + b * n_chunks + c

    in_specs = [
        pl.BlockSpec((L, hpg * dk), lambda b, h, c: (rows(b, c), h)),
        pl.BlockSpec((L, hpg * dk), lambda b, h, c: (rows(b, c), h)),
        pl.BlockSpec((L, hpg * dv), lambda b, h, c: (rows(b, c), h)),
        pl.BlockSpec((L, gates.shape[1]), lambda b, h, c: (rows(b, c), 0)),
    ]
    args = [q, k, v, gates]
    if gates_t.ndim == 2:
        in_specs.append(pl.BlockSpec((2 * heads, L), lambda b, h, c: (0, rows(b, c))))
    else:
        in_specs.append(pl.BlockSpec((None, 2 * heads, L), lambda b, h, c: (b, 0, 0)))
    args.append(gates_t)
    in_specs += [
        pl.BlockSpec((L, hpg * dv), lambda b, h, c: (rows(b, c), n_hg + h)),
        pl.BlockSpec((L, hpg * dv), lambda b, h, c: (rows(b, c), h)),
        pl.BlockSpec((L, hpg * dv), lambda b, h, c: (rows(b, c), h)),
        pl.BlockSpec((1, hpg * dv), lambda b, h, c: (0, h)),
        pl.BlockSpec((1, hpg * dv), lambda b, h, c: (0, h)),
    ]
    args += [zo, zo, xc, gn, skip]
    c_spec = pl.BlockSpec((1, hpg, dk, dv), lambda b, h, c: (b, h, 0, 0))
    n_spec = pl.BlockSpec((hpg, 1, dk), lambda b, h, c: (b * n_hg + h, 0, 0))
    m_spec = pl.BlockSpec((hpg, 1, 1), lambda b, h, c: (b * n_hg + h, 0, 0))
    if state is not None:
        in_specs += [c_spec, n_spec, m_spec]
        args += list(state)
    in_specs.append(pl.BlockSpec(memory_space=pl.ANY))
    args.append(x_out)
    return pl.pallas_call(
        functools.partial(_mlstm_chunk_kernel, has_state=state is not None, heads=heads, hpg=hpg, dk=dk, dv=dv),
        grid=(batch, n_hg, n_chunks),
        in_specs=in_specs,
        out_specs=[pl.BlockSpec((L, hpg * dv), lambda b, h, c: (rows(b, c), h)), c_spec, n_spec, m_spec],
        out_shape=[jax.ShapeDtypeStruct(x_out.shape, x_out.dtype),
                   jax.ShapeDtypeStruct((batch, heads, dk, dv), F32),
                   jax.ShapeDtypeStruct((batch * heads, 1, dk), F32),
                   jax.ShapeDtypeStruct((batch * heads, 1, 1), F32)],
        input_output_aliases={len(args) - 1: 0},
        compiler_params=_cparams(("parallel", "parallel", "arbitrary"), 48),
        name=name,
    )(*args)


def _mlstm_decode_kernel(q_ref, k_ref, v_ref, g_ref, op_ref, z_ref, xc_ref, gn_ref, sk_ref,
                         c_ref, n_ref, m_ref, o_ref, co_ref, no_ref, mo_ref, qt_ref, kt_ref, xcf_ref,
                         *, hpg, heads, dk, dv):
    hg = pl.program_id(0)
    b = pl.program_id(1)

    @pl.when(b == 0)
    def _():
        qt_ref[...] = q_ref[...].T
        kt_ref[...] = k_ref[...].T
        xcf_ref[...] = xc_ref[...].astype(F32)

    n_dec = q_ref.shape[0]
    onehot = lax.broadcasted_iota(jnp.int32, (1, n_dec), 1) == b
    grow = g_ref[pl.ds(b, 1), :]
    lane = lax.broadcasted_iota(jnp.int32, grow.shape, 1)
    m_all = m_ref[0]
    mlane = lax.broadcasted_iota(jnp.int32, m_all.shape, 1)
    scale = dk ** -0.5
    m_out = jnp.zeros((1, hpg), F32)
    olane = lax.broadcasted_iota(jnp.int32, (1, hpg), 1)
    for hh in range(hpg):
        hd = hg * hpg + hh
        logi = _select_lane(grow, lane == hd)
        logf = _select_lane(grow, lane == hd + heads)
        m_old = _select_lane(m_all, mlane == hd)
        inter = logf + m_old
        m_new = jnp.maximum(inter, logi)
        w = jnp.exp(logi - m_new)
        s = jnp.exp(inter - m_new)
        kcol = _select_lane(kt_ref[hh * dk:(hh + 1) * dk, :], onehot) * scale
        qcol = _select_lane(qt_ref[hh * dk:(hh + 1) * dk, :], onehot)
        krow = k_ref[pl.ds(b, 1), hh * dk:(hh + 1) * dk] * scale
        qrow = q_ref[pl.ds(b, 1), hh * dk:(hh + 1) * dk]
        vrow = v_ref[pl.ds(b, 1), hh * dv:(hh + 1) * dv]
        c_new = s * c_ref[0, 0, hh] + (w * kcol) * vrow
        co_ref[0, 0, hh] = c_new
        n_new = s * n_ref[0, :, hh * dk:(hh + 1) * dk] + w * krow
        no_ref[0, :, hh * dk:(hh + 1) * dk] = n_new
        m_out = jnp.where(olane == hh, m_new, m_out)
        num = jnp.sum(qcol * c_new, axis=0, keepdims=True)
        den = jnp.sum(qrow * n_new, axis=1, keepdims=True)
        h = num / jnp.maximum(jnp.abs(den), jnp.exp(-m_new))
        cols = slice(hh * dv, (hh + 1) * dv)
        out = _mlstm_head_out(h, op_ref[pl.ds(b, 1), cols], z_ref[pl.ds(b, 1), cols],
                              xcf_ref[pl.ds(b, 1), cols], gn_ref[:, cols], sk_ref[:, cols])
        o_ref[pl.ds(b, 1), cols] = out
    mo_ref[0, 0] = m_out


def _mlstm_decode(q, k, v, gates, zo, xc, gn, skip, c_state, n_state, m_state, *,
                  heads, dk, dv, dec_block, hpg):
    n_dec = c_state.shape[1]
    n_hg = heads // hpg
    inner = heads * dv
    n3 = n_state.reshape(n_dec, 1, heads * dk)
    m3 = m_state.reshape(n_dec, 1, heads)
    c_spec = pl.BlockSpec((1, 1, hpg, dk, dv), lambda hg, b: (0, b, hg, 0, 0))
    n_spec = pl.BlockSpec((1, 1, hpg * dk), lambda hg, b: (b, 0, hg))
    o, c_new, n_new, m_new = pl.pallas_call(
        functools.partial(_mlstm_decode_kernel, hpg=hpg, heads=heads, dk=dk, dv=dv),
        grid=(n_hg, n_dec),
        in_specs=[
            pl.BlockSpec((n_dec, hpg * dk), lambda hg, b: (dec_block, hg)),
            pl.BlockSpec((n_dec, hpg * dk), lambda hg, b: (dec_block, hg)),
            pl.BlockSpec((n_dec, hpg * dv), lambda hg, b: (dec_block, hg)),
            pl.BlockSpec((n_dec, gates.shape[1]), lambda hg, b: (dec_block, 0)),
            pl.BlockSpec((n_dec, hpg * dv), lambda hg, b: (dec_block, n_hg + hg)),
            pl.BlockSpec((n_dec, hpg * dv), lambda hg, b: (dec_block, hg)),
            pl.BlockSpec((n_dec, hpg * dv), lambda hg, b: (dec_block, hg)),
            pl.BlockSpec((1, hpg * dv), lambda hg, b: (0, hg)),
            pl.BlockSpec((1, hpg * dv), lambda hg, b: (0, hg)),
            c_spec, n_spec,
            pl.BlockSpec((1, 1, heads), lambda hg, b: (b, 0, 0)),
        ],
        out_specs=[pl.BlockSpec((n_dec, hpg * dv), lambda hg, b: (0, hg)), c_spec, n_spec,
                   pl.BlockSpec((1, 1, 1, hpg), lambda hg, b: (hg, b, 0, 0))],
        out_shape=[jax.ShapeDtypeStruct((n_dec, inner), F32),
                   jax.ShapeDtypeStruct(c_state.shape, F32),
                   jax.ShapeDtypeStruct(n3.shape, F32),
                   jax.ShapeDtypeStruct((n_hg, n_dec, 1, hpg), F32)],
        scratch_shapes=[pltpu.VMEM((hpg * dk, n_dec), F32), pltpu.VMEM((hpg * dk, n_dec), F32),
                        pltpu.VMEM((n_dec, hpg * dv), F32)],
        compiler_params=_cparams(("arbitrary", "arbitrary"), 48),
        name="mlstm_decode",
    )(q, k, v, gates, zo, zo, xc, gn, skip, c_state, n3, m3)
    m_new = jnp.transpose(m_new[:, :, 0, :], (1, 0, 2)).reshape(n_dec, heads)
    return o, c_new, n_new.reshape(n_state.shape), m_new.reshape(m_state.shape)


def kernel(x_prompt, x_sample, state_ret_S, state_mlstm_C, state_mlstm_n, state_mlstm_m, state_mlstm_conv,
           meta_tokens, norm_g, final_norm_g, ret_w_in, ret_w_out,
           ml_w_in, ml_conv_w, ml_conv_b, ml_w_q, ml_w_k, ml_w_v, ml_w_if, ml_b_if, ml_skip, ml_gn_g, ml_w_out):
    batch, seq, d = x_prompt.shape
    n_dec = x_sample.shape[0]
    n_tok = batch * seq
    n_chunks = seq // CHUNK
    dec0 = n_tok
    meta0 = n_tok + n_dec
    n_meta = batch * N_META
    m = -(-(meta0 + n_meta) // ROW_ALIGN) * ROW_ALIGN
    assert seq % CHUNK == 0 and n_dec == CHUNK and x_sample.shape[1] == 1
    assert state_ret_S.shape[0] == 1 and state_mlstm_C.shape[0] == 1
    dec_block = dec0 // n_dec

    ret_dk = d // RET_HEADS
    ret_dv = 2 * d // RET_HEADS
    ret_vw = RET_HEADS * ret_dv
    inner = 2 * d
    m_dk = d // M_HEADS
    m_dv = inner // M_HEADS

    bm = _pick(m, (1408, 768, 512, 384, 256))
    ret_hpg = min(4, RET_HEADS)
    m_hpg = min(2, M_HEADS)
    wide_kw = dict(bm=_pick(m, (768, 384, 256)), bn=512, n_sub=2, w_single_buffer=True, vmem_mib=57)

    h0 = jnp.concatenate([
        x_prompt.reshape(n_tok, d),
        x_sample.reshape(n_dec, d),
        jnp.broadcast_to(meta_tokens[None].astype(F32), (batch, N_META, d)).reshape(n_meta, d),
        jnp.zeros((m - meta0 - n_meta, d), F32)], axis=0)

    half = ret_dk // 2
    inv = ROPE_BASE ** (-jnp.arange(half, dtype=F32) / half)
    pos = jnp.concatenate([
        jnp.tile(N_META + jnp.arange(seq, dtype=jnp.int32), batch),
        jnp.full((n_dec,), PAST_LEN, jnp.int32),
        jnp.tile(jnp.arange(N_META, dtype=jnp.int32), batch),
        jnp.zeros((m - meta0 - n_meta,), jnp.int32)])
    ang = pos.astype(F32)[:, None] * inv[None, :]
    cos, sin = jnp.cos(ang), jnp.sin(ang)

    x0 = _rmsnorm(h0, norm_g[0], out_dtype=BF16, block_rows=ROW_ALIGN)
    bn = 512
    in_kw = dict(bm=bm, bn=bn, n_sub=4, vmem_mib=56)
    qk = _matmul(x0, ret_w_in[0], n_cols=2 * d, cos=cos, sin=sin,
                 rope=(d // bn, ret_dk, ret_dk ** -0.5), name="ret_in_proj_qk", **in_kw)
    vg = _matmul(x0, ret_w_in[0], col0=2 * d, act=(ret_vw // bn, "id", "silu"), name="ret_in_proj_vg", **in_kw)
    log_g = jnp.log1p(-jnp.exp2(-5.0 - jnp.arange(RET_HEADS, dtype=F32)))
    xr = jnp.zeros((m, ret_vw), BF16)
    xr, s_meta = _ret_chunks(log_g, qk, vg, xr, None, batch=batch, heads=RET_HEADS, dk=ret_dk, dv=ret_dv,
                             L=N_META, n_chunks=1, row0=meta0, hpg=ret_hpg, name="ret_meta")
    xr, s_prompt = _ret_chunks(log_g, qk, vg, xr, s_meta, batch=batch, heads=RET_HEADS, dk=ret_dk, dv=ret_dv,
                               L=CHUNK, n_chunks=n_chunks, row0=0, hpg=ret_hpg, name="ret_chunks")
    o_dec, s_sample = _ret_decode(log_g, qk, vg, state_ret_S, heads=RET_HEADS, dk=ret_dk, dv=ret_dv,
                                  dec_block=dec_block, hpg=min(8, RET_HEADS))
    xr = lax.dynamic_update_slice(xr, o_dec.astype(BF16), (dec0, 0))
    h1 = _matmul(xr, ret_w_out[0], res=h0, name="ret_out_proj", **wide_kw)

    x1 = _rmsnorm(h1, norm_g[1], out_dtype=BF16, block_rows=ROW_ALIGN)
    xm = _matmul(x1, ml_w_in[0], n_cols=inner, name="ml_in_proj_x", **in_kw)
    zo = _matmul(x1, ml_w_in[0], col0=inner, act=(inner // bn, "silu", "sigmoid"), name="ml_in_proj_zo", **in_kw)

    xm_meta = xm[meta0:meta0 + n_meta].reshape(batch, N_META, inner)
    conv_state = state_mlstm_conv[0]
    pad_rows = jnp.zeros((m - meta0 - n_meta, inner), F32)

    def tail_prev(kk):
        meta_prev = jnp.pad(xm_meta, ((0, 0), (kk, 0), (0, 0)))[:, :N_META].reshape(n_meta, inner)
        return jnp.concatenate([conv_state[:, CONV_W - 1 - kk], meta_prev, pad_rows], axis=0)

    x_tail = tuple(tail_prev(kk) for kk in range(1, CONV_W))
    xc, v = _conv_v(xm, x_tail, ml_conv_w[0], ml_conv_b[0].reshape(1, inner), _v_blockdiag(ml_w_v[0]),
                    n_tok=n_tok, seq=seq, meta0=meta0, inner=inner, ct=min(inner, 2048))

    q = _matmul(xc, ml_w_q[0], name="ml_q_proj", **wide_kw)
    k = _matmul(xc, ml_w_k[0], name="ml_k_proj", **wide_kw)
    gates, gates_t = _gates(q, k, v, ml_w_if[0], ml_b_if[0], heads=M_HEADS, bm=128)

    gn = ml_gn_g[0].reshape(1, inner)
    skip = ml_skip[0].reshape(1, inner)
    xm_out = jnp.zeros((m, inner), BF16)
    gt_meta = gates_t[:, meta0:meta0 + n_meta].reshape(2 * M_HEADS, batch, N_META).transpose(1, 0, 2)
    xm_out, c_meta, n_meta_s, m_meta = _mlstm_chunks(
        q, k, v, gates, gt_meta, zo, xc, gn, skip, xm_out, None,
        batch=batch, heads=M_HEADS, dk=m_dk, dv=m_dv, L=N_META, n_chunks=1, row0=meta0, hpg=m_hpg,
        name="mlstm_meta")
    xm_out, c_prompt, n_prompt, m_prompt = _mlstm_chunks(
        q, k, v, gates, gates_t, zo, xc, gn, skip, xm_out, (c_meta, n_meta_s, m_meta),
        batch=batch, heads=M_HEADS, dk=m_dk, dv=m_dv, L=CHUNK, n_chunks=n_chunks, row0=0, hpg=m_hpg,
        name="mlstm_chunks")
    h_dec, c_sample, n_sample, m_sample = _mlstm_decode(
        q, k, v, gates, zo, xc, gn, skip, state_mlstm_C, state_mlstm_n, state_mlstm_m,
        heads=M_HEADS, dk=m_dk, dv=m_dv, dec_block=dec_block, hpg=min(2, M_HEADS))
    xm_out = lax.dynamic_update_slice(xm_out, h_dec.astype(BF16), (dec0, 0))
    h2 = _matmul(xm_out, ml_w_out[0], res=h1, name="ml_out_proj", **wide_kw)

    y_prompt = _rmsnorm(h2, final_norm_g, out_dtype=F32, block_rows=CHUNK, n_blocks=n_tok // CHUNK)
    y_sample = _rmsnorm(h2, final_norm_g, out_dtype=F32, block_rows=CHUNK, first_block=dec_block, n_blocks=1)
    conv_prompt = jnp.stack([xm[(b + 1) * seq - (CONV_W - 1):(b + 1) * seq] for b in range(batch)])
    conv_sample = jnp.concatenate([conv_state[:, 1:], xm[dec0:dec0 + n_dec, None]], axis=1)
    return (y_prompt.reshape(batch, seq, d), y_sample.reshape(n_dec, 1, d),
            s_prompt[None], s_sample,
            c_prompt[None], c_sample,
            n_prompt.reshape(1, batch, M_HEADS, m_dk), n_sample,
            m_prompt.reshape(1, batch, M_HEADS), m_sample,
            conv_prompt[None], conv_sample[None])
```

```python
import functools

import jax
import jax.numpy as jnp
from jax import lax
from jax.experimental import pallas as pl
from jax.experimental.pallas import tpu as pltpu

F32 = jnp.float32
BF16 = jnp.bfloat16

N_META = 16
CHUNK = 128
EPS = 1e-6
RET_HEADS = 16
M_HEADS = 8
CONV_W = 4
V_BLOCK = 4
ROPE_BASE = 10000.0
PAST_LEN = 16384

ROW_ALIGN = 256
MIB = 1024 * 1024
NT_DIMS = (((1,), (1,)), ((), ()))
TN_DIMS = (((0,), (0,)), ((), ()))


def _cparams(sem, vmem_mib=None):
    kw = dict(dimension_semantics=sem)
    if vmem_mib is not None:
        kw["vmem_limit_bytes"] = int(vmem_mib * MIB)
    return pltpu.CompilerParams(**kw)


def _pick(n, candidates):
    for c in candidates:
        if n % c == 0:
            return c
    raise ValueError(f"no tile for {n} in {candidates}")


def _sigmoid(x):
    return 1.0 / (1.0 + jnp.exp(-x))


def _silu(x):
    return x * _sigmoid(x)


def _rmsnorm_kernel(x_ref, g_ref, o_ref):
    x = x_ref[...]
    y = x * lax.rsqrt(jnp.mean(x * x, axis=-1, keepdims=True) + EPS)
    o_ref[...] = (y * g_ref[...]).astype(o_ref.dtype)


def _rmsnorm(x, g, *, out_dtype, block_rows, first_block=0, n_blocks=None):
    m, d = x.shape
    if n_blocks is None:
        n_blocks = m // block_rows
    return pl.pallas_call(
        _rmsnorm_kernel,
        grid=(n_blocks,),
        in_specs=[pl.BlockSpec((block_rows, d), lambda i: (i + first_block, 0)),
                  pl.BlockSpec((1, d), lambda i: (0, 0))],
        out_specs=pl.BlockSpec((block_rows, d), lambda i: (i, 0)),
        out_shape=jax.ShapeDtypeStruct((n_blocks * block_rows, d), out_dtype),
        compiler_params=_cparams(("parallel",)),
        name="rmsnorm",
    )(x, g.reshape(1, d))


def _matmul_kernel(x_ref, w_ref, *rest, rope, act, has_res, n_sub, x_stationary):
    pos = 0
    if rope is not None:
        cos_ref, sin_ref = rest[0], rest[1]
        pos = 2
    if has_res:
        res_ref = rest[pos]
        pos += 1
    o_ref = rest[pos]
    if x_stationary:
        j = pl.program_id(1)
        wb = w_ref[...].astype(BF16)
    else:
        wb_ref = rest[pos + 1]
        j = pl.program_id(0)

        @pl.when(pl.program_id(1) == 0)
        def _():
            wb_ref[...] = w_ref[...].astype(BF16)

    sub = x_ref.shape[0] // n_sub
    for r in range(n_sub):
        rows = slice(r * sub, (r + 1) * sub)
        acc = jnp.dot(x_ref[rows, :], wb if x_stationary else wb_ref[...], preferred_element_type=F32)
        if has_res:
            acc = res_ref[rows, :] + acc
        if act is not None:
            n_first, first, rest_kind = act
            sig = 0.5 * jnp.tanh(0.5 * acc) + 0.5
            vals = {"id": acc, "silu": acc * sig, "sigmoid": sig}
            acc = jnp.where(j < n_first, vals[first], vals[rest_kind])
        if rope is None:
            o_ref[rows, :] = acc.astype(o_ref.dtype)
            continue
        n_q, head_dim, k_scale = rope
        half = head_dim // 2
        scale = jnp.where(j < n_q, 1.0, k_scale).astype(F32)
        cos = cos_ref[rows, :]
        sin = sin_ref[rows, :]
        for h in range(o_ref.shape[1] // head_dim):
            lo = h * head_dim
            x1 = acc[:, lo:lo + half]
            x2 = acc[:, lo + half:lo + head_dim]
            o_ref[rows, lo:lo + half] = ((x1 * cos - x2 * sin) * scale).astype(o_ref.dtype)
            o_ref[rows, lo + half:lo + head_dim] = ((x1 * sin + x2 * cos) * scale).astype(o_ref.dtype)


def _matmul(x, w, *, bm, bn, col0=0, n_cols=None, out_dtype=F32, res=None, rope=None, cos=None, sin=None,
            act=None, n_sub=1, x_stationary=False, single_buffer=False, vmem_mib=48, name="matmul"):
    m, k = x.shape
    n = w.shape[1] - col0 if n_cols is None else n_cols
    j0 = col0 // bn
    if x_stationary:
        grid = (m // bm, n // bn)
        ij = lambda a, b: (a, b)
    else:
        grid = (n // bn, m // bm)
        ij = lambda a, b: (b, a)
    one = dict(pipeline_mode=pl.Buffered(1)) if single_buffer else {}
    x_kw, w_kw = (one, {}) if x_stationary else ({}, one)
    in_specs = [pl.BlockSpec((bm, k), lambda a, b: (ij(a, b)[0], 0), **x_kw),
                pl.BlockSpec((k, bn), lambda a, b: (0, ij(a, b)[1] + j0), **w_kw)]
    args = [x, w]
    if rope is not None:
        half = rope[1] // 2
        in_specs += [pl.BlockSpec((bm, half), lambda a, b: (ij(a, b)[0], 0)),
                     pl.BlockSpec((bm, half), lambda a, b: (ij(a, b)[0], 0))]
        args += [cos, sin]
    if res is not None:
        in_specs.append(pl.BlockSpec((bm, bn), lambda a, b: ij(a, b)))
        args.append(res)
    return pl.pallas_call(
        functools.partial(_matmul_kernel, rope=rope, act=act, has_res=res is not None, n_sub=n_sub,
                          x_stationary=x_stationary),
        grid=grid,
        in_specs=in_specs,
        out_specs=pl.BlockSpec((bm, bn), lambda a, b: ij(a, b)),
        out_shape=jax.ShapeDtypeStruct((m, n), out_dtype),
        scratch_shapes=[] if x_stationary else [pltpu.VMEM((k, bn), BF16)],
        compiler_params=_cparams(("arbitrary", "arbitrary"), vmem_mib),
        name=name,
    )(*args)


def _ret_head_out(o, gate):
    o = o * lax.rsqrt(jnp.mean(o * o, axis=-1, keepdims=True) + EPS)
    return gate * o


def _ret_chunk_kernel(lg_ref, q_ref, k_ref, v_ref, g_ref, *rest, has_s0, hpg, dk, dv):
    if has_s0:
        s0_ref, _, o_ref, s_ref = rest
    else:
        _, o_ref, s_ref = rest
    hg = pl.program_id(1)

    @pl.when(pl.program_id(2) == 0)
    def _():
        if has_s0:
            s_ref[...] = s0_ref[...]
        else:
            s_ref[...] = jnp.zeros_like(s_ref)

    L = q_ref.shape[0]
    ii = lax.broadcasted_iota(jnp.int32, (L, L), 0)
    jj = lax.broadcasted_iota(jnp.int32, (L, L), 1)
    causal = ii >= jj
    diff = jnp.where(causal, (ii - jj).astype(F32), 0.0)
    idx = lax.broadcasted_iota(jnp.int32, (L, 1), 0).astype(F32)
    for hh in range(hpg):
        lg = jnp.full((1, 1), lg_ref[hg * hpg + hh], F32)
        q = q_ref[:, hh * dk:(hh + 1) * dk].astype(BF16)
        k = k_ref[:, hh * dk:(hh + 1) * dk]
        v = v_ref[:, hh * dv:(hh + 1) * dv].astype(BF16)
        s = s_ref[0, hh]
        decay = jnp.where(causal, jnp.exp(lg * diff), 0.0)
        scores = lax.dot_general(q, k.astype(BF16), NT_DIMS, preferred_element_type=F32) * decay
        inner = jnp.dot(scores.astype(BF16), v, preferred_element_type=F32)
        q_dec = jnp.exp(lg * (idx + 1.0))
        cross = jnp.dot(q, s.astype(BF16), preferred_element_type=F32) * q_dec
        k_dec = jnp.exp(lg * (L - 1.0 - idx))
        kd = (k.astype(F32) * k_dec).astype(BF16)
        s_ref[0, hh] = s * jnp.exp(lg * float(L)) + lax.dot_general(kd, v, TN_DIMS, preferred_element_type=F32)
        out = _ret_head_out(inner + cross, g_ref[:, hh * dv:(hh + 1) * dv].astype(F32))
        o_ref[:, hh * dv:(hh + 1) * dv] = out.astype(o_ref.dtype)


def _ret_chunks(log_g, qk, vg, x_out, s0, *, batch, heads, dk, dv, L, n_chunks, row0, hpg, name):
    rb0 = row0 // L
    n_hg = heads // hpg

    def rows(b, c):
        return rb0 + b * n_chunks + c

    s_spec = pl.BlockSpec((1, hpg, dk, dv), lambda b, h, c: (b, h, 0, 0))
    in_specs = [
        pl.BlockSpec(memory_space=pltpu.SMEM),
        pl.BlockSpec((L, hpg * dk), lambda b, h, c: (rows(b, c), h)),
        pl.BlockSpec((L, hpg * dk), lambda b, h, c: (rows(b, c), n_hg + h)),
        pl.BlockSpec((L, hpg * dv), lambda b, h, c: (rows(b, c), h)),
        pl.BlockSpec((L, hpg * dv), lambda b, h, c: (rows(b, c), n_hg + h)),
    ]
    args = [log_g, qk, qk, vg, vg]
    if s0 is not None:
        in_specs.append(s_spec)
        args.append(s0)
    in_specs.append(pl.BlockSpec(memory_space=pl.ANY))
    args.append(x_out)
    return pl.pallas_call(
        functools.partial(_ret_chunk_kernel, has_s0=s0 is not None, hpg=hpg, dk=dk, dv=dv),
        grid=(batch, n_hg, n_chunks),
        in_specs=in_specs,
        out_specs=[pl.BlockSpec((L, hpg * dv), lambda b, h, c: (rows(b, c), h)), s_spec],
        out_shape=[jax.ShapeDtypeStruct(x_out.shape, x_out.dtype),
                   jax.ShapeDtypeStruct((batch, heads, dk, dv), F32)],
        input_output_aliases={len(args) - 1: 0},
        compiler_params=_cparams(("parallel", "parallel", "arbitrary"), 48),
        name=name,
    )(*args)


def _select_lane(x, lane_onehot):
    return jnp.sum(jnp.where(lane_onehot, x, 0.0), axis=1, keepdims=True)


def _row_times_matrix(row, mat):
    lhs = jnp.broadcast_to(row, (8, row.shape[1])).astype(BF16)
    return jnp.dot(lhs, mat.astype(BF16), preferred_element_type=F32)[0:1, :]


def _ret_decode_kernel(lg_ref, q_ref, k_ref, v_ref, g_ref, s_ref, o_ref, so_ref, qf_ref, kt_ref, vf_ref, gf_ref,
                       *, hpg, dk, dv):
    hg = pl.program_id(0)
    b = pl.program_id(1)

    @pl.when(b == 0)
    def _():
        qf_ref[...] = q_ref[...].astype(F32)
        kt_ref[...] = k_ref[...].astype(F32).T
        vf_ref[...] = v_ref[...].astype(F32)
        gf_ref[...] = g_ref[...].astype(F32)

    n_dec = q_ref.shape[0]
    onehot = lax.broadcasted_iota(jnp.int32, (1, n_dec), 1) == b
    for hh in range(hpg):
        gamma = jnp.exp(jnp.full((1, 1), lg_ref[hg * hpg + hh], F32))
        kcol = _select_lane(kt_ref[hh * dk:(hh + 1) * dk, :], onehot)
        qrow = qf_ref[pl.ds(b, 1), hh * dk:(hh + 1) * dk]
        vrow = vf_ref[pl.ds(b, 1), hh * dv:(hh + 1) * dv]
        grow = gf_ref[pl.ds(b, 1), hh * dv:(hh + 1) * dv]
        s_new = s_ref[0, 0, hh] * gamma + kcol * vrow
        so_ref[0, 0, hh] = s_new
        o = _row_times_matrix(qrow, s_new)
        o_ref[pl.ds(b, 1), hh * dv:(hh + 1) * dv] = _ret_head_out(o, grow)


def _ret_decode(log_g, qk, vg, state, *, heads, dk, dv, dec_block, hpg):
    n_dec = state.shape[1]
    n_hg = heads // hpg
    state_spec = pl.BlockSpec((1, 1, hpg, dk, dv), lambda hg, b: (0, b, hg, 0, 0))
    return pl.pallas_call(
        functools.partial(_ret_decode_kernel, hpg=hpg, dk=dk, dv=dv),
        grid=(n_hg, n_dec),
        in_specs=[
            pl.BlockSpec(memory_space=pltpu.SMEM),
            pl.BlockSpec((n_dec, hpg * dk), lambda hg, b: (dec_block, hg)),
            pl.BlockSpec((n_dec, hpg * dk), lambda hg, b: (dec_block, n_hg + hg)),
            pl.BlockSpec((n_dec, hpg * dv), lambda hg, b: (dec_block, hg)),
            pl.BlockSpec((n_dec, hpg * dv), lambda hg, b: (dec_block, n_hg + hg)),
            state_spec,
        ],
        out_specs=[pl.BlockSpec((n_dec, hpg * dv), lambda hg, b: (0, hg)), state_spec],
        out_shape=[jax.ShapeDtypeStruct((n_dec, heads * dv), F32),
                   jax.ShapeDtypeStruct(state.shape, F32)],
        scratch_shapes=[pltpu.VMEM((n_dec, hpg * dk), F32), pltpu.VMEM((hpg * dk, n_dec), F32),
                        pltpu.VMEM((n_dec, hpg * dv), F32), pltpu.VMEM((n_dec, hpg * dv), F32)],
        compiler_params=_cparams(("arbitrary", "arbitrary"), 48),
        name="ret_decode",
    )(log_g, qk, qk, vg, vg, state)


LANES = 128


def _conv_v_compute(x0, x1, x2, x3, cw_ref, cb_ref, whi_ref, wlo_ref, xc_ref, v_ref):
    pre = cb_ref[...] + x3 * cw_ref[0:1, :] + x2 * cw_ref[1:2, :] + x1 * cw_ref[2:3, :] + x0 * cw_ref[3:4, :]
    xc_ref[...] = _silu(pre).astype(xc_ref.dtype)
    xh = x0.astype(BF16)
    xl = (x0 - xh.astype(F32)).astype(BF16)
    for c in range(x0.shape[1] // LANES):
        cols = slice(c * LANES, (c + 1) * LANES)
        whi = whi_ref[c]
        v = jnp.dot(xh[:, cols], whi, preferred_element_type=F32)
        v = v + jnp.dot(xl[:, cols], whi, preferred_element_type=F32)
        v = v + jnp.dot(xh[:, cols], wlo_ref[c], preferred_element_type=F32)
        v_ref[:, cols] = v.astype(v_ref.dtype)


def _conv_v_main_kernel(x_ref, prev_ref, cw_ref, cb_ref, whi_ref, wlo_ref, xc_ref, v_ref, ext_ref):
    rows = x_ref.shape[0]
    x0 = x_ref[...]
    ext_ref[0:8, :] = prev_ref[...]
    ext_ref[8:8 + rows, :] = x0
    x1 = ext_ref[7:7 + rows, :]
    x2 = ext_ref[6:6 + rows, :]
    x3 = ext_ref[5:5 + rows, :]
    _conv_v_compute(x0, x1, x2, x3, cw_ref, cb_ref, whi_ref, wlo_ref, xc_ref, v_ref)


def _conv_v_tail_kernel(x0_ref, x1_ref, x2_ref, x3_ref, cw_ref, cb_ref, whi_ref, wlo_ref, _, __, xc_ref, v_ref):
    _conv_v_compute(x0_ref[...], x1_ref[...], x2_ref[...], x3_ref[...], cw_ref, cb_ref, whi_ref, wlo_ref,
                    xc_ref, v_ref)


def _v_blockdiag(w_v):
    g = LANES // V_BLOCK
    w = w_v.reshape(w_v.shape[0] // g, g, V_BLOCK, V_BLOCK)
    bd = jnp.einsum('bgcd,gh->bgchd', w, jnp.eye(g, dtype=w_v.dtype)).reshape(-1, LANES, LANES)
    hi = bd.astype(BF16)
    lo = (bd - hi.astype(F32)).astype(BF16)
    return hi, lo


def _conv_v(xm, x_tail, conv_w, conv_b, w_bd, *, n_tok, seq, meta0, inner, ct):
    m = xm.shape[0]
    rb = CHUNK
    blocks_per_seq = seq // rb
    n_ct = inner // ct

    def prev_block(i):
        b = i // blocks_per_seq
        first = (meta0 + b * N_META + N_META - 8) // 8
        return jnp.where(i % blocks_per_seq == 0, first, i * (rb // 8) - 1)

    nb = ct // LANES
    w_specs = [pl.BlockSpec((CONV_W, ct), lambda j, i: (0, j)),
               pl.BlockSpec((1, ct), lambda j, i: (0, j)),
               pl.BlockSpec((nb, LANES, LANES), lambda j, i: (j, 0, 0)),
               pl.BlockSpec((nb, LANES, LANES), lambda j, i: (j, 0, 0))]
    out_shape = [jax.ShapeDtypeStruct((m, inner), BF16), jax.ShapeDtypeStruct((m, inner), BF16)]
    xc, v = pl.pallas_call(
        _conv_v_main_kernel,
        grid=(n_ct, n_tok // rb),
        in_specs=[pl.BlockSpec((rb, ct), lambda j, i: (i, j)),
                  pl.BlockSpec((8, ct), lambda j, i: (prev_block(i), j))] + w_specs,
        out_specs=[pl.BlockSpec((rb, ct), lambda j, i: (i, j))] * 2,
        out_shape=out_shape,
        scratch_shapes=[pltpu.VMEM((rb + 8, ct), F32)],
        compiler_params=_cparams(("parallel", "parallel")),
        name="conv_v_main",
    )(xm, xm, conv_w, conv_b, *w_bd)

    n_tail = m - n_tok
    tb0 = n_tok // n_tail
    tail_in = pl.BlockSpec((n_tail, ct), lambda j: (0, j))
    tail_out = pl.BlockSpec((n_tail, ct), lambda j: (tb0, j))
    w_specs1 = [pl.BlockSpec((CONV_W, ct), lambda j: (0, j)),
                pl.BlockSpec((1, ct), lambda j: (0, j)),
                pl.BlockSpec((nb, LANES, LANES), lambda j: (j, 0, 0)),
                pl.BlockSpec((nb, LANES, LANES), lambda j: (j, 0, 0))]
    any_spec = pl.BlockSpec(memory_space=pl.ANY)
    xc, v = pl.pallas_call(
        _conv_v_tail_kernel,
        grid=(n_ct,),
        in_specs=[tail_out, tail_in, tail_in, tail_in] + w_specs1 + [any_spec, any_spec],
        out_specs=[tail_out, tail_out],
        out_shape=out_shape,
        input_output_aliases={8: 0, 9: 1},
        compiler_params=_cparams(("parallel",)),
        name="conv_v_tail",
    )(xm, *x_tail, conv_w, conv_b, *w_bd, xc, v)
    return xc, v


def _gates_kernel(q_ref, k_ref, v_ref, wq_ref, wk_ref, wv_ref, b_ref, g_ref, gt_ref, *, heads):
    acc = lax.dot_general(q_ref[...].astype(BF16), wq_ref[...].astype(BF16), NT_DIMS, preferred_element_type=F32)
    acc = acc + lax.dot_general(k_ref[...].astype(BF16), wk_ref[...].astype(BF16), NT_DIMS, preferred_element_type=F32)
    acc = acc + lax.dot_general(v_ref[...].astype(BF16), wv_ref[...].astype(BF16), NT_DIMS, preferred_element_type=F32)
    pre = acc + b_ref[...]
    col = lax.broadcasted_iota(jnp.int32, pre.shape, 1)
    log_sig = jnp.minimum(pre, 0.0) - jnp.log1p(jnp.exp(-jnp.abs(pre)))
    g = jnp.where(col >= heads, log_sig, pre)
    g_ref[...] = g
    gt_ref[...] = g.T[:gt_ref.shape[0], :]


def _gates(q, k, v, w_if, b_if, *, heads, bm):
    m, qk = q.shape
    inner = v.shape[1]
    lanes = 128
    w_t = jnp.zeros((lanes, w_if.shape[0]), F32).at[:2 * heads].set(w_if.T)
    bias = jnp.zeros((1, lanes), F32).at[0, :2 * heads].set(b_if)
    return pl.pallas_call(
        functools.partial(_gates_kernel, heads=heads),
        grid=(m // bm,),
        in_specs=[pl.BlockSpec((bm, qk), lambda i: (i, 0)),
                  pl.BlockSpec((bm, qk), lambda i: (i, 0)),
                  pl.BlockSpec((bm, inner), lambda i: (i, 0)),
                  pl.BlockSpec((lanes, qk), lambda i: (0, 0)),
                  pl.BlockSpec((lanes, qk), lambda i: (0, 1)),
                  pl.BlockSpec((lanes, inner), lambda i: (0, 2 * qk // inner)),
                  pl.BlockSpec((1, lanes), lambda i: (0, 0))],
        out_specs=[pl.BlockSpec((bm, lanes), lambda i: (i, 0)),
                   pl.BlockSpec((2 * heads, bm), lambda i: (0, i))],
        out_shape=[jax.ShapeDtypeStruct((m, lanes), F32), jax.ShapeDtypeStruct((2 * heads, m), F32)],
        compiler_params=_cparams(("parallel",), 48),
        name="gates",
    )(q, k, v, w_t, w_t, w_t, bias)


def _mlstm_head_out(h, o_gate, z_gate, xc, gn, skip):
    h = o_gate * h
    mu = jnp.mean(h, axis=-1, keepdims=True)
    var = jnp.mean(jnp.square(h - mu), axis=-1, keepdims=True)
    hn = (h - mu) * lax.rsqrt(var + EPS) * gn
    return (hn + skip * xc) * z_gate


def _mlstm_chunk_kernel(q_ref, k_ref, v_ref, g_ref, gt_ref, op_ref, z_ref, xc_ref, gn_ref, sk_ref, *rest,
                        has_state, heads, hpg, dk, dv):
    if has_state:
        c0_ref, n0_ref, m0_ref, _, o_ref, c_ref, n_ref, m_ref = rest
    else:
        _, o_ref, c_ref, n_ref, m_ref = rest
    hg = pl.program_id(1)

    @pl.when(pl.program_id(2) == 0)
    def _():
        if has_state:
            c_ref[...] = c0_ref[...]
            n_ref[...] = n0_ref[...]
            m_ref[...] = m0_ref[...]
        else:
            c_ref[...] = jnp.zeros_like(c_ref)
            n_ref[...] = jnp.zeros_like(n_ref)
            m_ref[...] = jnp.zeros_like(m_ref)

    L = q_ref.shape[0]
    gates = g_ref[...]
    lane = lax.broadcasted_iota(jnp.int32, (1, gates.shape[1]), 1)
    ii = lax.broadcasted_iota(jnp.int32, (L, L), 0)
    jj = lax.broadcasted_iota(jnp.int32, (L, L), 1)
    causal = ii >= jj
    for hh in range(hpg):
        hd = hg * hpg + hh
        logi_c = _select_lane(gates, lane == hd)
        logf_c = _select_lane(gates, lane == hd + heads)
        logi_r = gt_ref[pl.ds(hd, 1), :]
        logf_r = gt_ref[pl.ds(hd + heads, 1), :]
        bcum_c = jnp.sum(jnp.where(causal, logf_r, 0.0), axis=1, keepdims=True)
        bcum_r = jnp.sum(jnp.where(ii <= jj, logf_c, 0.0), axis=0, keepdims=True)
        total = jnp.sum(logf_r, axis=1, keepdims=True)

        m_old = m_ref[hh]
        dlog = jnp.where(causal, bcum_c - bcum_r + logi_r, -jnp.inf)
        inter = bcum_c + m_old
        m_row = jnp.maximum(inter, jnp.max(dlog, axis=1, keepdims=True))
        w = jnp.exp(dlog - m_row)
        s_inter = jnp.exp(inter - m_row)

        q = q_ref[:, hh * dk:(hh + 1) * dk].astype(F32)
        ks = k_ref[:, hh * dk:(hh + 1) * dk].astype(F32) * (dk ** -0.5)
        qb = q.astype(BF16)
        cols = slice(hh * dv, (hh + 1) * dv)
        vb = v_ref[:, cols].astype(BF16)
        c_old = c_ref[0, hh]
        n_old = n_ref[hh]

        qk = lax.dot_general(qb, ks.astype(BF16), NT_DIMS, preferred_element_type=F32) * w
        num = jnp.dot(qk.astype(BF16), vb, preferred_element_type=F32)
        num = num + s_inter * jnp.dot(qb, c_old.astype(BF16), preferred_element_type=F32)
        den = jnp.sum(qk, axis=1, keepdims=True) + s_inter * jnp.sum(q * n_old, axis=1, keepdims=True)
        h = num / jnp.maximum(jnp.abs(den), jnp.exp(-m_row))

        m_new = m_row[L - 1:L, :]
        wk = jnp.exp(total - bcum_c + logi_c - m_new)
        s_c = jnp.exp(total + m_old - m_new)
        kw = ks * wk
        c_ref[0, hh] = s_c * c_old + lax.dot_general(kw.astype(BF16), vb, TN_DIMS, preferred_element_type=F32)
        n_ref[hh] = s_c * n_old + jnp.sum(kw, axis=0, keepdims=True)
        m_ref[hh] = m_new

        out = _mlstm_head_out(h, op_ref[:, cols].astype(F32), z_ref[:, cols].astype(F32), xc_ref[:, cols].astype(F32),
                              gn_ref[:, cols], sk_ref[:, cols])
        o_ref[:, cols] = out.astype(o_ref.dtype)


def _mlstm_chunks(q, k, v, gates, gates_t, zo, xc, gn, skip, x_out, state, *,
                  batch, heads, dk, dv, L, n_chunks, row0, hpg, name):
    rb0 = row0 // L
    n_hg = heads // hpg

    def rows(b, c):
        return rb0 + b * n_chunks + c

    in_specs = [
        pl.BlockSpec((L, hpg * dk), lambda b, h, c: (rows(b, c), h)),
        pl.BlockSpec((L, hpg * dk), lambda b, h, c: (rows(b, c), h)),
        pl.BlockSpec((L, hpg * dv), lambda b, h, c: (rows(b, c), h)),
        pl.BlockSpec((L, gates.shape[1]), lambda b, h, c: (rows(b, c), 0)),
    ]
    args = [q, k, v, gates]
    if gates_t.ndim == 2:
        in_specs.append(pl.BlockSpec((2 * heads, L), lambda b, h, c: (0, rows(b, c))))
    else:
        in_specs.append(pl.BlockSpec((None, 2 * heads, L), lambda b, h, c: (b, 0, 0)))
    args.append(gates_t)
    in_specs += [
        pl.BlockSpec((L, hpg * dv), lambda b, h, c: (rows(b, c), n_hg + h)),
        pl.BlockSpec((L, hpg * dv), lambda b, h, c: (rows(b, c), h)),
        pl.BlockSpec((L, hpg * dv), lambda b, h, c: (rows(b, c), h)),
        pl.BlockSpec((1, hpg * dv), lambda b, h, c: (0, h)),
        pl.BlockSpec((1, hpg * dv), lambda b, h, c: (0, h)),
    ]
    args += [zo, zo, xc, gn, skip]
    c_spec = pl.BlockSpec((1, hpg, dk, dv), lambda b, h, c: (b, h, 0, 0))
    n_spec = pl.BlockSpec((hpg, 1, dk), lambda b, h, c: (b * n_hg + h, 0, 0))
    m_spec = pl.BlockSpec((hpg, 1, 1), lambda b, h, c: (b * n_hg + h, 0, 0))
    if state is not None:
        in_specs += [c_spec, n_spec, m_spec]
        args += list(state)
    in_specs.append(pl.BlockSpec(memory_space=pl.ANY))
    args.append(x_out)
    return pl.pallas_call(
        functools.partial(_mlstm_chunk_kernel, has_state=state is not None, heads=heads, hpg=hpg, dk=dk, dv=dv),
        grid=(batch, n_hg, n_chunks),
        in_specs=in_specs,
        out_specs=[pl.BlockSpec((L, hpg * dv), lambda b, h, c: (rows(b, c), h)), c_spec, n_spec, m_spec],
        out_shape=[jax.ShapeDtypeStruct(x_out.shape, x_out.dtype),
                   jax.ShapeDtypeStruct((batch, heads, dk, dv), F32),
                   jax.ShapeDtypeStruct((batch * heads, 1, dk), F32),
                   jax.ShapeDtypeStruct((batch * heads, 1, 1), F32)],
        input_output_aliases={len(args) - 1: 0},
        compiler_params=_cparams(("parallel", "parallel", "arbitrary"), 48),
        name=name,
    )(*args)


def _mlstm_decode_kernel(q_ref, k_ref, v_ref, g_ref, op_ref, z_ref, xc_ref, gn_ref, sk_ref,
                         c_ref, n_ref, m_ref, o_ref, co_ref, no_ref, mo_ref,
                         kt_ref, qf_ref, kf_ref, vf_ref, opf_ref, zf_ref, xcf_ref,
                         *, hpg, heads, dk, dv):
    hg = pl.program_id(0)
    b = pl.program_id(1)

    @pl.when(b == 0)
    def _():
        qf_ref[...] = q_ref[...].astype(F32)
        kf_ref[...] = k_ref[...].astype(F32)
        kt_ref[...] = kf_ref[...].T
        vf_ref[...] = v_ref[...].astype(F32)
        opf_ref[...] = op_ref[...].astype(F32)
        zf_ref[...] = z_ref[...].astype(F32)
        xcf_ref[...] = xc_ref[...].astype(F32)

    n_dec = q_ref.shape[0]
    onehot = lax.broadcasted_iota(jnp.int32, (1, n_dec), 1) == b
    grow = g_ref[pl.ds(b, 1), :]
    lane = lax.broadcasted_iota(jnp.int32, grow.shape, 1)
    m_all = m_ref[0]
    mlane = lax.broadcasted_iota(jnp.int32, m_all.shape, 1)
    scale = dk ** -0.5
    m_out = jnp.zeros((1, hpg), F32)
    olane = lax.broadcasted_iota(jnp.int32, (1, hpg), 1)
    for hh in range(hpg):
        hd = hg * hpg + hh
        logi = _select_lane(grow, lane == hd)
        logf = _select_lane(grow, lane == hd + heads)
        m_old = _select_lane(m_all, mlane == hd)
        inter = logf + m_old
        m_new = jnp.maximum(inter, logi)
        w = jnp.exp(logi - m_new)
        s = jnp.exp(inter - m_new)
        kcol = _select_lane(kt_ref[hh * dk:(hh + 1) * dk, :], onehot) * scale
        krow = kf_ref[pl.ds(b, 1), hh * dk:(hh + 1) * dk] * scale
        qrow = qf_ref[pl.ds(b, 1), hh * dk:(hh + 1) * dk]
        vrow = vf_ref[pl.ds(b, 1), hh * dv:(hh + 1) * dv]
        c_new = s * c_ref[0, 0, hh] + (w * kcol) * vrow
        co_ref[0, 0, hh] = c_new
        n_new = s * n_ref[0, :, hh * dk:(hh + 1) * dk] + w * krow
        no_ref[0, :, hh * dk:(hh + 1) * dk] = n_new
        m_out = jnp.where(olane == hh, m_new, m_out)
        num = _row_times_matrix(qrow, c_new)
        den = jnp.sum(qrow * n_new, axis=1, keepdims=True)
        h = num / jnp.maximum(jnp.abs(den), jnp.exp(-m_new))
        cols = slice(hh * dv, (hh + 1) * dv)
        out = _mlstm_head_out(h, opf_ref[pl.ds(b, 1), cols], zf_ref[pl.ds(b, 1), cols],
                              xcf_ref[pl.ds(b, 1), cols], gn_ref[:, cols], sk_ref[:, cols])
        o_ref[pl.ds(b, 1), cols] = out
    mo_ref[0, 0] = m_out


def _mlstm_decode(q, k, v, gates, zo, xc, gn, skip, c_state, n_state, m_state, *,
                  heads, dk, dv, dec_block, hpg):
    n_dec = c_state.shape[1]
    n_hg = heads // hpg
    inner = heads * dv
    n3 = n_state.reshape(n_dec, 1, heads * dk)
    m3 = m_state.reshape(n_dec, 1, heads)
    c_spec = pl.BlockSpec((1, 1, hpg, dk, dv), lambda hg, b: (0, b, hg, 0, 0))
    n_spec = pl.BlockSpec((1, 1, hpg * dk), lambda hg, b: (b, 0, hg))
    o, c_new, n_new, m_new = pl.pallas_call(
        functools.partial(_mlstm_decode_kernel, hpg=hpg, heads=heads, dk=dk, dv=dv),
        grid=(n_hg, n_dec),
        in_specs=[
            pl.BlockSpec((n_dec, hpg * dk), lambda hg, b: (dec_block, hg)),
            pl.BlockSpec((n_dec, hpg * dk), lambda hg, b: (dec_block, hg)),
            pl.BlockSpec((n_dec, hpg * dv), lambda hg, b: (dec_block, hg)),
            pl.BlockSpec((n_dec, gates.shape[1]), lambda hg, b: (dec_block, 0)),
            pl.BlockSpec((n_dec, hpg * dv), lambda hg, b: (dec_block, n_hg + hg)),
            pl.BlockSpec((n_dec, hpg * dv), lambda hg, b: (dec_block, hg)),
            pl.BlockSpec((n_dec, hpg * dv), lambda hg, b: (dec_block, hg)),
            pl.BlockSpec((1, hpg * dv), lambda hg, b: (0, hg)),
            pl.BlockSpec((1, hpg * dv), lambda hg, b: (0, hg)),
            c_spec, n_spec,
            pl.BlockSpec((1, 1, heads), lambda hg, b: (b, 0, 0)),
        ],
        out_specs=[pl.BlockSpec((n_dec, hpg * dv), lambda hg, b: (0, hg)), c_spec, n_spec,
                   pl.BlockSpec((1, 1, 1, hpg), lambda hg, b: (hg, b, 0, 0))],
        out_shape=[jax.ShapeDtypeStruct((n_dec, inner), F32),
                   jax.ShapeDtypeStruct(c_state.shape, F32),
                   jax.ShapeDtypeStruct(n3.shape, F32),
                   jax.ShapeDtypeStruct((n_hg, n_dec, 1, hpg), F32)],
        scratch_shapes=[pltpu.VMEM((hpg * dk, n_dec), F32),
                        pltpu.VMEM((n_dec, hpg * dk), F32), pltpu.VMEM((n_dec, hpg * dk), F32)]
        + [pltpu.VMEM((n_dec, hpg * dv), F32)] * 4,
        compiler_params=_cparams(("arbitrary", "arbitrary"), 48),
        name="mlstm_decode",
    )(q, k, v, gates, zo, zo, xc, gn, skip, c_state, n3, m3)
    m_new = jnp.transpose(m_new[:, :, 0, :], (1, 0, 2)).reshape(n_dec, heads)
    return o, c_new, n_new.reshape(n_state.shape), m_new.reshape(m_state.shape)


def kernel(x_prompt, x_sample, state_ret_S, state_mlstm_C, state_mlstm_n, state_mlstm_m, state_mlstm_conv,
           meta_tokens, norm_g, final_norm_g, ret_w_in, ret_w_out,
           ml_w_in, ml_conv_w, ml_conv_b, ml_w_q, ml_w_k, ml_w_v, ml_w_if, ml_b_if, ml_skip, ml_gn_g, ml_w_out):
    batch, seq, d = x_prompt.shape
    n_dec = x_sample.shape[0]
    n_tok = batch * seq
    n_chunks = seq // CHUNK
    dec0 = n_tok
    meta0 = n_tok + n_dec
    n_meta = batch * N_META
    m = -(-(meta0 + n_meta) // ROW_ALIGN) * ROW_ALIGN
    assert seq % CHUNK == 0 and n_dec == CHUNK and x_sample.shape[1] == 1
    assert state_ret_S.shape[0] == 1 and state_mlstm_C.shape[0] == 1
    dec_block = dec0 // n_dec

    ret_dk = d // RET_HEADS
    ret_dv = 2 * d // RET_HEADS
    ret_vw = RET_HEADS * ret_dv
    inner = 2 * d
    m_dk = d // M_HEADS
    m_dv = inner // M_HEADS

    bm = _pick(m, (2112, 1408, 768, 512, 384, 256))
    ret_hpg = min(8, RET_HEADS)
    m_hpg = min(2, M_HEADS)
    wide_kw = dict(bm=_pick(m, (1408, 768, 384, 256)), bn=256, n_sub=4, x_stationary=True, single_buffer=True,
                   vmem_mib=56)

    h0 = jnp.concatenate([
        x_prompt.reshape(n_tok, d),
        x_sample.reshape(n_dec, d),
        jnp.broadcast_to(meta_tokens[None].astype(F32), (batch, N_META, d)).reshape(n_meta, d),
        jnp.zeros((m - meta0 - n_meta, d), F32)], axis=0)

    half = ret_dk // 2
    inv = ROPE_BASE ** (-jnp.arange(half, dtype=F32) / half)
    pos = jnp.concatenate([
        jnp.tile(N_META + jnp.arange(seq, dtype=jnp.int32), batch),
        jnp.full((n_dec,), PAST_LEN, jnp.int32),
        jnp.tile(jnp.arange(N_META, dtype=jnp.int32), batch),
        jnp.zeros((m - meta0 - n_meta,), jnp.int32)])
    ang = pos.astype(F32)[:, None] * inv[None, :]
    cos, sin = jnp.cos(ang), jnp.sin(ang)

    x0 = _rmsnorm(h0, norm_g[0], out_dtype=BF16, block_rows=ROW_ALIGN)
    bn = 512
    in_kw = dict(bm=bm, bn=bn, n_sub=4, x_stationary=True, single_buffer=True, vmem_mib=56)
    qk = _matmul(x0, ret_w_in[0], n_cols=2 * d, cos=cos, sin=sin, out_dtype=BF16,
                 rope=(d // bn, ret_dk, ret_dk ** -0.5), name="ret_in_proj_qk", **in_kw)
    vg = _matmul(x0, ret_w_in[0], col0=2 * d, act=(ret_vw // bn, "id", "silu"), out_dtype=BF16,
                 name="ret_in_proj_vg", **in_kw)
    log_g = jnp.log1p(-jnp.exp2(-5.0 - jnp.arange(RET_HEADS, dtype=F32)))
    xr = jnp.zeros((m, ret_vw), BF16)
    xr, s_meta = _ret_chunks(log_g, qk, vg, xr, None, batch=batch, heads=RET_HEADS, dk=ret_dk, dv=ret_dv,
                             L=N_META, n_chunks=1, row0=meta0, hpg=ret_hpg, name="ret_meta")
    xr, s_prompt = _ret_chunks(log_g, qk, vg, xr, s_meta, batch=batch, heads=RET_HEADS, dk=ret_dk, dv=ret_dv,
                               L=CHUNK, n_chunks=n_chunks, row0=0, hpg=ret_hpg, name="ret_chunks")
    o_dec, s_sample = _ret_decode(log_g, qk, vg, state_ret_S, heads=RET_HEADS, dk=ret_dk, dv=ret_dv,
                                  dec_block=dec_block, hpg=min(8, RET_HEADS))
    xr = lax.dynamic_update_slice(xr, o_dec.astype(BF16), (dec0, 0))
    h1 = _matmul(xr, ret_w_out[0], res=h0, name="ret_out_proj", **wide_kw)

    x1 = _rmsnorm(h1, norm_g[1], out_dtype=BF16, block_rows=ROW_ALIGN)
    xm = _matmul(x1, ml_w_in[0], n_cols=inner, name="ml_in_proj_x", **in_kw)
    zo = _matmul(x1, ml_w_in[0], col0=inner, act=(inner // bn, "silu", "sigmoid"), name="ml_in_proj_zo", **in_kw)

    xm_meta = xm[meta0:meta0 + n_meta].reshape(batch, N_META, inner)
    conv_state = state_mlstm_conv[0]
    pad_rows = jnp.zeros((m - meta0 - n_meta, inner), F32)

    def tail_prev(kk):
        meta_prev = jnp.pad(xm_meta, ((0, 0), (kk, 0), (0, 0)))[:, :N_META].reshape(n_meta, inner)
        return jnp.concatenate([conv_state[:, CONV_W - 1 - kk], meta_prev, pad_rows], axis=0)

    x_tail = tuple(tail_prev(kk) for kk in range(1, CONV_W))
    xc, v = _conv_v(xm, x_tail, ml_conv_w[0], ml_conv_b[0].reshape(1, inner), _v_blockdiag(ml_w_v[0]),
                    n_tok=n_tok, seq=seq, meta0=meta0, inner=inner, ct=min(inner, 2048))

    q = _matmul(xc, ml_w_q[0], out_dtype=BF16, name="ml_q_proj", **wide_kw)
    k = _matmul(xc, ml_w_k[0], out_dtype=BF16, name="ml_k_proj", **wide_kw)
    gates, gates_t = _gates(q, k, v, ml_w_if[0], ml_b_if[0], heads=M_HEADS, bm=ROW_ALIGN)

    gn = ml_gn_g[0].reshape(1, inner)
    skip = ml_skip[0].reshape(1, inner)
    xm_out = jnp.zeros((m, inner), BF16)
    gt_meta = gates_t[:, meta0:meta0 + n_meta].reshape(2 * M_HEADS, batch, N_META).transpose(1, 0, 2)
    xm_out, c_meta, n_meta_s, m_meta = _mlstm_chunks(
        q, k, v, gates, gt_meta, zo, xc, gn, skip, xm_out, None,
        batch=batch, heads=M_HEADS, dk=m_dk, dv=m_dv, L=N_META, n_chunks=1, row0=meta0, hpg=m_hpg,
        name="mlstm_meta")
    xm_out, c_prompt, n_prompt, m_prompt = _mlstm_chunks(
        q, k, v, gates, gates_t, zo, xc, gn, skip, xm_out, (c_meta, n_meta_s, m_meta),
        batch=batch, heads=M_HEADS, dk=m_dk, dv=m_dv, L=CHUNK, n_chunks=n_chunks, row0=0, hpg=m_hpg,
        name="mlstm_chunks")
    h_dec, c_sample, n_sample, m_sample = _mlstm_decode(
        q, k, v, gates, zo, xc, gn, skip, state_mlstm_C, state_mlstm_n, state_mlstm_m,
        heads=M_HEADS, dk=m_dk, dv=m_dv, dec_block=dec_block, hpg=min(2, M_HEADS))
    xm_out = lax.dynamic_update_slice(xm_out, h_dec.astype(BF16), (dec0, 0))
    h2 = _matmul(xm_out, ml_w_out[0], res=h1, name="ml_out_proj", **wide_kw)

    y_prompt = _rmsnorm(h2, final_norm_g, out_dtype=F32, block_rows=CHUNK, n_blocks=n_tok // CHUNK)
    y_sample = _rmsnorm(h2, final_norm_g, out_dtype=F32, block_rows=CHUNK, first_block=dec_block, n_blocks=1)
    conv_prompt = jnp.stack([xm[(b + 1) * seq - (CONV_W - 1):(b + 1) * seq] for b in range(batch)])
    conv_sample = jnp.concatenate([conv_state[:, 1:], xm[dec0:dec0 + n_dec, None]], axis=1)
    return (y_prompt.reshape(batch, seq, d), y_sample.reshape(n_dec, 1, d),
            s_prompt[None], s_sample,
            c_prompt[None], c_sample,
            n_prompt.reshape(1, batch, M_HEADS, m_dk), n_sample,
            m_prompt.reshape(1, batch, M_HEADS), m_sample,
            conv_prompt[None], conv_sample[None])
```

```python
import functools

import jax
import jax.numpy as jnp
from jax import lax
from jax.experimental import pallas as pl
from jax.experimental.pallas import tpu as pltpu

F32 = jnp.float32
BF16 = jnp.bfloat16

N_META = 16
CHUNK = 128
EPS = 1e-6
RET_HEADS = 16
M_HEADS = 8
CONV_W = 4
V_BLOCK = 4
ROPE_BASE = 10000.0
PAST_LEN = 16384

ROW_ALIGN = 256
MIB = 1024 * 1024
NT_DIMS = (((1,), (1,)), ((), ()))
TN_DIMS = (((0,), (0,)), ((), ()))


def _cparams(sem, vmem_mib=None):
    kw = dict(dimension_semantics=sem)
    if vmem_mib is not None:
        kw["vmem_limit_bytes"] = int(vmem_mib * MIB)
    return pltpu.CompilerParams(**kw)


def _pick(n, candidates):
    for c in candidates:
        if n % c == 0:
            return c
    raise ValueError(f"no tile for {n} in {candidates}")


def _sigmoid(x):
    return 1.0 / (1.0 + jnp.exp(-x))


def _silu(x):
    return x * _sigmoid(x)


def _rmsnorm_kernel(x_ref, g_ref, o_ref):
    x = x_ref[...]
    y = x * lax.rsqrt(jnp.mean(x * x, axis=-1, keepdims=True) + EPS)
    o_ref[...] = (y * g_ref[...]).astype(o_ref.dtype)


def _rmsnorm_split_kernel(xa_ref, xb_ref, g_ref, o_ref, *, n_a):
    x = jnp.where(pl.program_id(0) < n_a, xa_ref[...], xb_ref[...])
    y = x * lax.rsqrt(jnp.mean(x * x, axis=-1, keepdims=True) + EPS)
    o_ref[...] = (y * g_ref[...]).astype(o_ref.dtype)


def _rmsnorm_split(xa, xb, g, *, out_dtype, block_rows):
    d = xa.shape[1]
    n_a, n_b = xa.shape[0] // block_rows, xb.shape[0] // block_rows
    return pl.pallas_call(
        functools.partial(_rmsnorm_split_kernel, n_a=n_a),
        grid=(n_a + n_b,),
        in_specs=[pl.BlockSpec((block_rows, d), lambda i: (jnp.minimum(i, n_a - 1), 0)),
                  pl.BlockSpec((block_rows, d), lambda i: (jnp.maximum(i - n_a, 0), 0)),
                  pl.BlockSpec((1, d), lambda i: (0, 0))],
        out_specs=pl.BlockSpec((block_rows, d), lambda i: (i, 0)),
        out_shape=jax.ShapeDtypeStruct(((n_a + n_b) * block_rows, d), out_dtype),
        compiler_params=_cparams(("parallel",)),
        name="rmsnorm_split",
    )(xa, xb, g.reshape(1, d))


def _rmsnorm(x, g, *, out_dtype, block_rows, first_block=0, n_blocks=None):
    m, d = x.shape
    if n_blocks is None:
        n_blocks = m // block_rows
    return pl.pallas_call(
        _rmsnorm_kernel,
        grid=(n_blocks,),
        in_specs=[pl.BlockSpec((block_rows, d), lambda i: (i + first_block, 0)),
                  pl.BlockSpec((1, d), lambda i: (0, 0))],
        out_specs=pl.BlockSpec((block_rows, d), lambda i: (i, 0)),
        out_shape=jax.ShapeDtypeStruct((n_blocks * block_rows, d), out_dtype),
        compiler_params=_cparams(("parallel",)),
        name="rmsnorm",
    )(x, g.reshape(1, d))


def _matmul_kernel(x_ref, w_ref, *rest, rope, act, has_res, res_split, n_sub):
    pos = 0
    if rope is not None:
        cos_ref, sin_ref = rest[0], rest[1]
        pos = 2
    if has_res:
        res_ref = rest[pos]
        pos += 1
    if res_split is not None:
        res_b_ref = rest[pos]
        pos += 1
    o_ref = rest[pos]
    i, j = pl.program_id(0), pl.program_id(1)
    wb = w_ref[...].astype(BF16)

    sub = x_ref.shape[0] // n_sub
    for r in range(n_sub):
        rows = slice(r * sub, (r + 1) * sub)
        acc = jnp.dot(x_ref[rows, :], wb, preferred_element_type=F32)
        if has_res:
            res = res_ref[rows, :]
            if res_split is not None:
                res = jnp.where(i < res_split, res, res_b_ref[rows, :])
            acc = res + acc
        if act is not None:
            n_first, first, rest_kind = act
            sig = 0.5 * jnp.tanh(0.5 * acc) + 0.5
            vals = {"id": acc, "silu": acc * sig, "sigmoid": sig}
            acc = jnp.where(j < n_first, vals[first], vals[rest_kind])
        if rope is None:
            o_ref[rows, :] = acc.astype(o_ref.dtype)
            continue
        n_q, head_dim, k_scale = rope
        half = head_dim // 2
        scale = jnp.where(j < n_q, 1.0, k_scale).astype(F32)
        cos = cos_ref[rows, :]
        sin = sin_ref[rows, :]
        for h in range(o_ref.shape[1] // head_dim):
            lo = h * head_dim
            x1 = acc[:, lo:lo + half]
            x2 = acc[:, lo + half:lo + head_dim]
            o_ref[rows, lo:lo + half] = ((x1 * cos - x2 * sin) * scale).astype(o_ref.dtype)
            o_ref[rows, lo + half:lo + head_dim] = ((x1 * sin + x2 * cos) * scale).astype(o_ref.dtype)


def _matmul(x, w, *, bm, bn, col0=0, n_cols=None, out_dtype=F32, res=None, rope=None, cos=None, sin=None,
            act=None, n_sub=1, vmem_mib=48, name="matmul"):
    m, k = x.shape
    n = w.shape[1] - col0 if n_cols is None else n_cols
    j0 = col0 // bn
    in_specs = [pl.BlockSpec((bm, k), lambda i, j: (i, 0), pipeline_mode=pl.Buffered(1)),
                pl.BlockSpec((k, bn), lambda i, j: (0, j + j0))]
    args = [x, w]
    if rope is not None:
        half = rope[1] // 2
        in_specs += [pl.BlockSpec((bm, half), lambda i, j: (i, 0)),
                     pl.BlockSpec((bm, half), lambda i, j: (i, 0))]
        args += [cos, sin]
    res_split = None
    if isinstance(res, tuple):
        res_a, res_b = res
        res_split = res_a.shape[0] // bm
        in_specs += [pl.BlockSpec((bm, bn), lambda i, j: (jnp.minimum(i, res_split - 1), j)),
                     pl.BlockSpec((bm, bn), lambda i, j: (jnp.maximum(i - res_split, 0), j))]
        args += [res_a, res_b]
    elif res is not None:
        in_specs.append(pl.BlockSpec((bm, bn), lambda i, j: (i, j)))
        args.append(res)
    return pl.pallas_call(
        functools.partial(_matmul_kernel, rope=rope, act=act, has_res=res is not None, res_split=res_split,
                          n_sub=n_sub),
        grid=(m // bm, n // bn),
        in_specs=in_specs,
        out_specs=pl.BlockSpec((bm, bn), lambda i, j: (i, j)),
        out_shape=jax.ShapeDtypeStruct((m, n), out_dtype),
        compiler_params=_cparams(("arbitrary", "arbitrary"), vmem_mib),
        name=name,
    )(*args)


def _ret_head_out(o, gate):
    o = o * lax.rsqrt(jnp.mean(o * o, axis=-1, keepdims=True) + EPS)
    return gate * o


def _for_real_groups(n_real, step, out_refs):
    if n_real is None:
        step()
        return
    pl.when(pl.program_id(0) < n_real)(step)

    @pl.when(pl.program_id(0) >= n_real)
    def _():
        for ref in out_refs:
            ref[...] = jnp.zeros_like(ref)


def _ret_chunk_kernel(lg_ref, q_ref, k_ref, v_ref, g_ref, *rest, has_s0, hpg, dk, dv, n_real):
    if has_s0:
        s0_ref, _, o_ref, s_ref = rest
    else:
        s0_ref = None
        _, o_ref, s_ref = rest
    step = functools.partial(_ret_chunk_step, lg_ref, q_ref, k_ref, v_ref, g_ref, s0_ref, o_ref, s_ref,
                             hpg=hpg, dk=dk, dv=dv)
    _for_real_groups(n_real, step, (o_ref, s_ref))


def _ret_chunk_step(lg_ref, q_ref, k_ref, v_ref, g_ref, s0_ref, o_ref, s_ref, *, hpg, dk, dv):
    hg = pl.program_id(1)

    @pl.when(pl.program_id(2) == 0)
    def _():
        if s0_ref is not None:
            s_ref[...] = s0_ref[...]
        else:
            s_ref[...] = jnp.zeros_like(s_ref)

    L = q_ref.shape[0]
    ii = lax.broadcasted_iota(jnp.int32, (L, L), 0)
    jj = lax.broadcasted_iota(jnp.int32, (L, L), 1)
    causal = ii >= jj
    diff = jnp.where(causal, (ii - jj).astype(F32), 0.0)
    idx = lax.broadcasted_iota(jnp.int32, (L, 1), 0).astype(F32)
    for hh in range(hpg):
        lg = jnp.full((1, 1), lg_ref[hg * hpg + hh], F32)
        q = q_ref[:, hh * dk:(hh + 1) * dk].astype(BF16)
        k = k_ref[:, hh * dk:(hh + 1) * dk]
        v = v_ref[:, hh * dv:(hh + 1) * dv].astype(BF16)
        s = s_ref[0, hh]
        decay = jnp.where(causal, jnp.exp(lg * diff), 0.0)
        scores = lax.dot_general(q, k.astype(BF16), NT_DIMS, preferred_element_type=F32) * decay
        inner = jnp.dot(scores.astype(BF16), v, preferred_element_type=F32)
        q_dec = jnp.exp(lg * (idx + 1.0))
        cross = jnp.dot(q, s.astype(BF16), preferred_element_type=F32) * q_dec
        k_dec = jnp.exp(lg * (L - 1.0 - idx))
        kd = (k.astype(F32) * k_dec).astype(BF16)
        s_ref[0, hh] = s * jnp.exp(lg * float(L)) + lax.dot_general(kd, v, TN_DIMS, preferred_element_type=F32)
        out = _ret_head_out(inner + cross, g_ref[:, hh * dv:(hh + 1) * dv].astype(F32))
        o_ref[:, hh * dv:(hh + 1) * dv] = out.astype(o_ref.dtype)


def _ret_chunks(log_g, qk, vg, x_out, s0, *, batch, heads, dk, dv, L, n_chunks, row0, hpg, name, n_real=None):
    rb0 = row0 // L
    n_hg = heads // hpg

    def rows(b, c):
        return rb0 + b * n_chunks + c

    s_spec = pl.BlockSpec((1, hpg, dk, dv), lambda b, h, c: (b, h, 0, 0))
    in_specs = [
        pl.BlockSpec(memory_space=pltpu.SMEM),
        pl.BlockSpec((L, hpg * dk), lambda b, h, c: (rows(b, c), h)),
        pl.BlockSpec((L, hpg * dk), lambda b, h, c: (rows(b, c), n_hg + h)),
        pl.BlockSpec((L, hpg * dv), lambda b, h, c: (rows(b, c), h)),
        pl.BlockSpec((L, hpg * dv), lambda b, h, c: (rows(b, c), n_hg + h)),
    ]
    args = [log_g, qk, qk, vg, vg]
    if s0 is not None:
        in_specs.append(s_spec)
        args.append(s0)
    in_specs.append(pl.BlockSpec(memory_space=pl.ANY))
    args.append(x_out)
    return pl.pallas_call(
        functools.partial(_ret_chunk_kernel, has_s0=s0 is not None, hpg=hpg, dk=dk, dv=dv, n_real=n_real),
        grid=(batch, n_hg, n_chunks),
        in_specs=in_specs,
        out_specs=[pl.BlockSpec((L, hpg * dv), lambda b, h, c: (rows(b, c), h)), s_spec],
        out_shape=[jax.ShapeDtypeStruct(x_out.shape, x_out.dtype),
                   jax.ShapeDtypeStruct((batch, heads, dk, dv), F32)],
        input_output_aliases={len(args) - 1: 0},
        compiler_params=_cparams(("parallel", "parallel", "arbitrary"), 48),
        name=name,
    )(*args)


def _select_lane(x, lane_onehot):
    return jnp.sum(jnp.where(lane_onehot, x, 0.0), axis=1, keepdims=True)


def _row_times_matrix(row, mat):
    lhs = jnp.broadcast_to(row, (8, row.shape[1])).astype(BF16)
    return jnp.dot(lhs, mat.astype(BF16), preferred_element_type=F32)[0:1, :]


def _ret_decode_kernel(lg_ref, q_ref, k_ref, v_ref, g_ref, s_ref, _, o_ref, so_ref,
                       qf_ref, kt_ref, vf_ref, gf_ref, of_ref, *, hpg, dk, dv):
    hg = pl.program_id(0)
    b = pl.program_id(1)

    @pl.when(b == 0)
    def _():
        qf_ref[...] = q_ref[...].astype(F32)
        kt_ref[...] = k_ref[...].astype(F32).T
        vf_ref[...] = v_ref[...].astype(F32)
        gf_ref[...] = g_ref[...].astype(F32)

    n_dec = q_ref.shape[0]
    onehot = lax.broadcasted_iota(jnp.int32, (1, n_dec), 1) == b
    for hh in range(hpg):
        gamma = jnp.exp(jnp.full((1, 1), lg_ref[hg * hpg + hh], F32))
        kcol = _select_lane(kt_ref[hh * dk:(hh + 1) * dk, :], onehot)
        qrow = qf_ref[pl.ds(b, 1), hh * dk:(hh + 1) * dk]
        vrow = vf_ref[pl.ds(b, 1), hh * dv:(hh + 1) * dv]
        grow = gf_ref[pl.ds(b, 1), hh * dv:(hh + 1) * dv]
        s_new = s_ref[0, 0, hh] * gamma + kcol * vrow
        so_ref[0, 0, hh] = s_new
        o = _row_times_matrix(qrow, s_new)
        of_ref[pl.ds(b, 1), hh * dv:(hh + 1) * dv] = _ret_head_out(o, grow)

    @pl.when(b == n_dec - 1)
    def _():
        o_ref[...] = of_ref[...].astype(o_ref.dtype)


def _ret_decode(log_g, qk, vg, state, x_out, *, heads, dk, dv, dec_block, hpg):
    n_dec = state.shape[1]
    n_hg = heads // hpg
    state_spec = pl.BlockSpec((1, 1, hpg, dk, dv), lambda hg, b: (0, b, hg, 0, 0))
    return pl.pallas_call(
        functools.partial(_ret_decode_kernel, hpg=hpg, dk=dk, dv=dv),
        grid=(n_hg, n_dec),
        in_specs=[
            pl.BlockSpec(memory_space=pltpu.SMEM),
            pl.BlockSpec((n_dec, hpg * dk), lambda hg, b: (dec_block, hg)),
            pl.BlockSpec((n_dec, hpg * dk), lambda hg, b: (dec_block, n_hg + hg)),
            pl.BlockSpec((n_dec, hpg * dv), lambda hg, b: (dec_block, hg)),
            pl.BlockSpec((n_dec, hpg * dv), lambda hg, b: (dec_block, n_hg + hg)),
            state_spec,
            pl.BlockSpec(memory_space=pl.ANY),
        ],
        out_specs=[pl.BlockSpec((n_dec, hpg * dv), lambda hg, b: (dec_block, hg)), state_spec],
        out_shape=[jax.ShapeDtypeStruct(x_out.shape, x_out.dtype),
                   jax.ShapeDtypeStruct(state.shape, F32)],
        scratch_shapes=[pltpu.VMEM((n_dec, hpg * dk), F32), pltpu.VMEM((hpg * dk, n_dec), F32)]
        + [pltpu.VMEM((n_dec, hpg * dv), F32)] * 3,
        input_output_aliases={6: 0},
        compiler_params=_cparams(("arbitrary", "arbitrary"), 48),
        name="ret_decode",
    )(log_g, qk, qk, vg, vg, state, x_out)


LANES = 128


def _conv_v_compute(x0, x1, x2, x3, cw_ref, cb_ref, whi_ref, wlo_ref, xc_ref, v_ref):
    pre = cb_ref[...] + x3 * cw_ref[0:1, :] + x2 * cw_ref[1:2, :] + x1 * cw_ref[2:3, :] + x0 * cw_ref[3:4, :]
    xc_ref[...] = _silu(pre).astype(xc_ref.dtype)
    xh = x0.astype(BF16)
    for c in range(x0.shape[1] // LANES):
        cols = slice(c * LANES, (c + 1) * LANES)
        v = jnp.dot(xh[:, cols], whi_ref[c], preferred_element_type=F32)
        v = v + jnp.dot(xh[:, cols], wlo_ref[c], preferred_element_type=F32)
        v_ref[:, cols] = v.astype(v_ref.dtype)


def _conv_v_kernel(x_ref, prev_ref, x1_ref, x2_ref, x3_ref, cw_ref, cb_ref, whi_ref, wlo_ref, xc_ref, v_ref,
                   *, n_prompt_blocks):
    i = pl.program_id(1)

    @pl.when(i < n_prompt_blocks)
    def _():
        x0 = x_ref[...]
        halo = prev_ref.shape[0]
        ext = jnp.concatenate([prev_ref[...], x0], axis=0)
        x1, x2, x3 = (pltpu.roll(ext, k, 0)[halo:, :] for k in range(1, CONV_W))
        _conv_v_compute(x0, x1, x2, x3, cw_ref, cb_ref, whi_ref, wlo_ref, xc_ref, v_ref)

    @pl.when(i >= n_prompt_blocks)
    def _():
        _conv_v_compute(x_ref[...], x1_ref[...], x2_ref[...], x3_ref[...], cw_ref, cb_ref, whi_ref, wlo_ref,
                        xc_ref, v_ref)


def _v_blockdiag(w_v):
    g = LANES // V_BLOCK
    w = w_v.reshape(w_v.shape[0] // g, g, V_BLOCK, V_BLOCK)
    bd = jnp.einsum('bgcd,gh->bgchd', w, jnp.eye(g, dtype=w_v.dtype)).reshape(-1, LANES, LANES)
    hi = bd.astype(BF16)
    lo = (bd - hi.astype(F32)).astype(BF16)
    return hi, lo


def _conv_v(xm, x_tail, conv_w, conv_b, w_bd, *, n_tok, seq, meta0, inner, ct):
    m = xm.shape[0]
    rb = ROW_ALIGN
    blocks_per_seq = seq // rb
    n_ct = inner // ct

    n_prompt_blocks = n_tok // rb

    def prev_block(i):
        b = i // blocks_per_seq
        first = (meta0 + b * N_META + N_META - 8) // 8
        prev = jnp.where(i % blocks_per_seq == 0, first, i * (rb // 8) - 1)
        return jnp.where(i < n_prompt_blocks, prev, 0)

    def tail_block(i):
        return jnp.maximum(i - n_prompt_blocks, 0)

    nb = ct // LANES
    tail_spec = pl.BlockSpec((rb, ct), lambda j, i: (tail_block(i), j))
    return pl.pallas_call(
        functools.partial(_conv_v_kernel, n_prompt_blocks=n_prompt_blocks),
        grid=(n_ct, m // rb),
        in_specs=[pl.BlockSpec((rb, ct), lambda j, i: (i, j)),
                  pl.BlockSpec((8, ct), lambda j, i: (prev_block(i), j)),
                  tail_spec, tail_spec, tail_spec,
                  pl.BlockSpec((CONV_W, ct), lambda j, i: (0, j)),
                  pl.BlockSpec((1, ct), lambda j, i: (0, j)),
                  pl.BlockSpec((nb, LANES, LANES), lambda j, i: (j, 0, 0)),
                  pl.BlockSpec((nb, LANES, LANES), lambda j, i: (j, 0, 0))],
        out_specs=[pl.BlockSpec((rb, ct), lambda j, i: (i, j))] * 2,
        out_shape=[jax.ShapeDtypeStruct((m, inner), BF16), jax.ShapeDtypeStruct((m, inner), BF16)],
        compiler_params=_cparams(("parallel", "parallel")),
        name="conv_v",
    )(xm, xm, *x_tail, conv_w, conv_b, *w_bd)


def _gates_kernel(q_ref, k_ref, v_ref, wq_ref, wk_ref, wv_ref, b_ref, g_ref, gt_ref, *, heads):
    acc = lax.dot_general(q_ref[...].astype(BF16), wq_ref[...].astype(BF16), NT_DIMS, preferred_element_type=F32)
    acc = acc + lax.dot_general(k_ref[...].astype(BF16), wk_ref[...].astype(BF16), NT_DIMS, preferred_element_type=F32)
    acc = acc + lax.dot_general(v_ref[...].astype(BF16), wv_ref[...].astype(BF16), NT_DIMS, preferred_element_type=F32)
    pre = acc + b_ref[...]
    col = lax.broadcasted_iota(jnp.int32, pre.shape, 1)
    log_sig = jnp.minimum(pre, 0.0) - jnp.log1p(jnp.exp(-jnp.abs(pre)))
    g = jnp.where(col >= heads, log_sig, pre)
    g_ref[...] = g
    gt_ref[...] = g.T[:gt_ref.shape[0], :]


def _gates(q, k, v, w_if, b_if, *, heads, bm):
    m, qk = q.shape
    inner = v.shape[1]
    lanes = 128
    w_t = jnp.zeros((lanes, w_if.shape[0]), F32).at[:2 * heads].set(w_if.T)
    bias = jnp.zeros((1, lanes), F32).at[0, :2 * heads].set(b_if)
    return pl.pallas_call(
        functools.partial(_gates_kernel, heads=heads),
        grid=(m // bm,),
        in_specs=[pl.BlockSpec((bm, qk), lambda i: (i, 0)),
                  pl.BlockSpec((bm, qk), lambda i: (i, 0)),
                  pl.BlockSpec((bm, inner), lambda i: (i, 0)),
                  pl.BlockSpec((lanes, qk), lambda i: (0, 0)),
                  pl.BlockSpec((lanes, qk), lambda i: (0, 1)),
                  pl.BlockSpec((lanes, inner), lambda i: (0, 2 * qk // inner)),
                  pl.BlockSpec((1, lanes), lambda i: (0, 0))],
        out_specs=[pl.BlockSpec((bm, lanes), lambda i: (i, 0)),
                   pl.BlockSpec((2 * heads, bm), lambda i: (0, i))],
        out_shape=[jax.ShapeDtypeStruct((m, lanes), F32), jax.ShapeDtypeStruct((2 * heads, m), F32)],
        compiler_params=_cparams(("parallel",), 48),
        name="gates",
    )(q, k, v, w_t, w_t, w_t, bias)


def _mlstm_head_out(h, o_gate, z_gate, xc, gn, skip):
    h = o_gate * h
    mu = jnp.mean(h, axis=-1, keepdims=True)
    var = jnp.mean(jnp.square(h - mu), axis=-1, keepdims=True)
    hn = (h - mu) * lax.rsqrt(var + EPS) * gn
    return (hn + skip * xc) * z_gate


def _mlstm_chunk_kernel(q_ref, k_ref, v_ref, g_ref, gt_ref, op_ref, z_ref, xc_ref, gn_ref, sk_ref, *rest,
                        has_state, heads, hpg, dk, dv, n_real):
    if has_state:
        state0 = rest[:3]
        rest = rest[3:]
    else:
        state0 = None
    step = functools.partial(_mlstm_chunk_step, q_ref, k_ref, v_ref, g_ref, gt_ref, op_ref, z_ref, xc_ref,
                             gn_ref, sk_ref, state0, *rest, heads=heads, hpg=hpg, dk=dk, dv=dv)
    _for_real_groups(n_real, step, rest)


def _mlstm_chunk_step(q_ref, k_ref, v_ref, g_ref, gt_ref, op_ref, z_ref, xc_ref, gn_ref, sk_ref, state0,
                      o_ref, c_ref, n_ref, m_ref, *, heads, hpg, dk, dv):
    hg = pl.program_id(1)

    @pl.when(pl.program_id(2) == 0)
    def _():
        if state0 is not None:
            c0_ref, n0_ref, m0_ref = state0
            c_ref[...] = c0_ref[...]
            n_ref[...] = n0_ref[...]
            m_ref[...] = m0_ref[...]
        else:
            c_ref[...] = jnp.zeros_like(c_ref)
            n_ref[...] = jnp.zeros_like(n_ref)
            m_ref[...] = jnp.zeros_like(m_ref)

    L = q_ref.shape[0]
    gates = g_ref[...]
    lane = lax.broadcasted_iota(jnp.int32, (1, gates.shape[1]), 1)
    ii = lax.broadcasted_iota(jnp.int32, (L, L), 0)
    jj = lax.broadcasted_iota(jnp.int32, (L, L), 1)
    causal = ii >= jj
    for hh in range(hpg):
        hd = hg * hpg + hh
        logi_c = _select_lane(gates, lane == hd)
        logf_c = _select_lane(gates, lane == hd + heads)
        logi_r = gt_ref[pl.ds(hd, 1), :]
        logf_r = gt_ref[pl.ds(hd + heads, 1), :]
        bcum_c = jnp.sum(jnp.where(causal, logf_r, 0.0), axis=1, keepdims=True)
        bcum_r = jnp.sum(jnp.where(ii <= jj, logf_c, 0.0), axis=0, keepdims=True)
        total = jnp.sum(logf_r, axis=1, keepdims=True)

        m_old = m_ref[hh]
        dlog = jnp.where(causal, bcum_c - bcum_r + logi_r, -jnp.inf)
        inter = bcum_c + m_old
        m_row = jnp.maximum(inter, jnp.max(dlog, axis=1, keepdims=True))
        w = jnp.exp(dlog - m_row)
        s_inter = jnp.exp(inter - m_row)

        q = q_ref[:, hh * dk:(hh + 1) * dk].astype(F32)
        ks = k_ref[:, hh * dk:(hh + 1) * dk].astype(F32) * (dk ** -0.5)
        qb = q.astype(BF16)
        cols = slice(hh * dv, (hh + 1) * dv)
        vb = v_ref[:, cols].astype(BF16)
        c_old = c_ref[0, hh]
        n_old = n_ref[hh]

        qk = lax.dot_general(qb, ks.astype(BF16), NT_DIMS, preferred_element_type=F32) * w
        num = jnp.dot(qk.astype(BF16), vb, preferred_element_type=F32)
        num = num + s_inter * jnp.dot(qb, c_old.astype(BF16), preferred_element_type=F32)
        den = jnp.sum(qk, axis=1, keepdims=True) + s_inter * jnp.sum(q * n_old, axis=1, keepdims=True)
        h = num / jnp.maximum(jnp.abs(den), jnp.exp(-m_row))

        m_new = m_row[L - 1:L, :]
        wk = jnp.exp(total - bcum_c + logi_c - m_new)
        s_c = jnp.exp(total + m_old - m_new)
        kw = ks * wk
        c_ref[0, hh] = s_c * c_old + lax.dot_general(kw.astype(BF16), vb, TN_DIMS, preferred_element_type=F32)
        n_ref[hh] = s_c * n_old + jnp.sum(kw, axis=0, keepdims=True)
        m_ref[hh] = m_new

        out = _mlstm_head_out(h, op_ref[:, cols].astype(F32), z_ref[:, cols].astype(F32), xc_ref[:, cols].astype(F32),
                              gn_ref[:, cols], sk_ref[:, cols])
        o_ref[:, cols] = out.astype(o_ref.dtype)


def _mlstm_chunks(q, k, v, gates, gates_t, zo, xc, gn, skip, state, *,
                  batch, heads, dk, dv, L, n_chunks, row0, hpg, name, n_real=None):
    rb0 = row0 // L
    n_hg = heads // hpg

    def rows(b, c):
        return rb0 + b * n_chunks + c

    in_specs = [
        pl.BlockSpec((L, hpg * dk), lambda b, h, c: (rows(b, c), h)),
        pl.BlockSpec((L, hpg * dk), lambda b, h, c: (rows(b, c), h)),
        pl.BlockSpec((L, hpg * dv), lambda b, h, c: (rows(b, c), h)),
        pl.BlockSpec((L, gates.shape[1]), lambda b, h, c: (rows(b, c), 0)),
    ]
    args = [q, k, v, gates]
    if gates_t.ndim == 2:
        in_specs.append(pl.BlockSpec((2 * heads, L), lambda b, h, c: (0, rows(b, c))))
    else:
        in_specs.append(pl.BlockSpec((None, 2 * heads, L), lambda b, h, c: (b, 0, 0)))
    args.append(gates_t)
    in_specs += [
        pl.BlockSpec((L, hpg * dv), lambda b, h, c: (rows(b, c), n_hg + h)),
        pl.BlockSpec((L, hpg * dv), lambda b, h, c: (rows(b, c), h)),
        pl.BlockSpec((L, hpg * dv), lambda b, h, c: (rows(b, c), h)),
        pl.BlockSpec((1, hpg * dv), lambda b, h, c: (0, h)),
        pl.BlockSpec((1, hpg * dv), lambda b, h, c: (0, h)),
    ]
    xc_pos = len(args) + 2
    args += [zo, zo, xc, gn, skip]
    c_spec = pl.BlockSpec((1, hpg, dk, dv), lambda b, h, c: (b, h, 0, 0))
    n_spec = pl.BlockSpec((hpg, 1, dk), lambda b, h, c: (b * n_hg + h, 0, 0))
    m_spec = pl.BlockSpec((hpg, 1, 1), lambda b, h, c: (b * n_hg + h, 0, 0))
    if state is not None:
        in_specs += [c_spec, n_spec, m_spec]
        args += list(state)
    return pl.pallas_call(
        functools.partial(_mlstm_chunk_kernel, has_state=state is not None, heads=heads, hpg=hpg, dk=dk, dv=dv,
                          n_real=n_real),
        grid=(batch, n_hg, n_chunks),
        in_specs=in_specs,
        out_specs=[pl.BlockSpec((L, hpg * dv), lambda b, h, c: (rows(b, c), h)), c_spec, n_spec, m_spec],
        out_shape=[jax.ShapeDtypeStruct(xc.shape, xc.dtype),
                   jax.ShapeDtypeStruct((batch, heads, dk, dv), F32),
                   jax.ShapeDtypeStruct((batch * heads, 1, dk), F32),
                   jax.ShapeDtypeStruct((batch * heads, 1, 1), F32)],
        input_output_aliases={xc_pos: 0},
        compiler_params=_cparams(("parallel", "parallel", "arbitrary"), 48),
        name=name,
    )(*args)


def _mlstm_decode_kernel(q_ref, k_ref, v_ref, g_ref, op_ref, z_ref, xc_ref, gn_ref, sk_ref,
                         c_ref, n_ref, m_ref, o_ref, co_ref, no_ref, mo_ref,
                         kt_ref, qf_ref, kf_ref, vf_ref, opf_ref, zf_ref, xcf_ref, of_ref,
                         *, hpg, heads, dk, dv):
    hg = pl.program_id(0)
    b = pl.program_id(1)

    @pl.when(b == 0)
    def _():
        qf_ref[...] = q_ref[...].astype(F32)
        kf_ref[...] = k_ref[...].astype(F32)
        kt_ref[...] = kf_ref[...].T
        vf_ref[...] = v_ref[...].astype(F32)
        opf_ref[...] = op_ref[...].astype(F32)
        zf_ref[...] = z_ref[...].astype(F32)
        xcf_ref[...] = xc_ref[...].astype(F32)

    n_dec = q_ref.shape[0]
    onehot = lax.broadcasted_iota(jnp.int32, (1, n_dec), 1) == b
    grow = g_ref[pl.ds(b, 1), :]
    lane = lax.broadcasted_iota(jnp.int32, grow.shape, 1)
    m_all = m_ref[0]
    mlane = lax.broadcasted_iota(jnp.int32, m_all.shape, 1)
    scale = dk ** -0.5
    m_out = jnp.zeros((1, hpg), F32)
    olane = lax.broadcasted_iota(jnp.int32, (1, hpg), 1)
    for hh in range(hpg):
        hd = hg * hpg + hh
        logi = _select_lane(grow, lane == hd)
        logf = _select_lane(grow, lane == hd + heads)
        m_old = _select_lane(m_all, mlane == hd)
        inter = logf + m_old
        m_new = jnp.maximum(inter, logi)
        w = jnp.exp(logi - m_new)
        s = jnp.exp(inter - m_new)
        kcol = _select_lane(kt_ref[hh * dk:(hh + 1) * dk, :], onehot) * scale
        krow = kf_ref[pl.ds(b, 1), hh * dk:(hh + 1) * dk] * scale
        qrow = qf_ref[pl.ds(b, 1), hh * dk:(hh + 1) * dk]
        vrow = vf_ref[pl.ds(b, 1), hh * dv:(hh + 1) * dv]
        c_new = s * c_ref[0, 0, hh] + (w * kcol) * vrow
        co_ref[0, 0, hh] = c_new
        n_new = s * n_ref[0, :, hh * dk:(hh + 1) * dk] + w * krow
        no_ref[0, :, hh * dk:(hh + 1) * dk] = n_new
        m_out = jnp.where(olane == hh, m_new, m_out)
        num = _row_times_matrix(qrow, c_new)
        den = jnp.sum(qrow * n_new, axis=1, keepdims=True)
        h = num / jnp.maximum(jnp.abs(den), jnp.exp(-m_new))
        cols = slice(hh * dv, (hh + 1) * dv)
        out = _mlstm_head_out(h, opf_ref[pl.ds(b, 1), cols], zf_ref[pl.ds(b, 1), cols],
                              xcf_ref[pl.ds(b, 1), cols], gn_ref[:, cols], sk_ref[:, cols])
        of_ref[pl.ds(b, 1), cols] = out
    mo_ref[0, 0] = m_out

    @pl.when(b == n_dec - 1)
    def _():
        o_ref[...] = of_ref[...].astype(o_ref.dtype)


def _mlstm_decode(q, k, v, gates, zo, xc, gn, skip, c_state, n_state, m_state, *,
                  heads, dk, dv, dec_block, hpg):
    n_dec = c_state.shape[1]
    n_hg = heads // hpg
    inner = heads * dv
    n3 = n_state.reshape(n_dec, 1, heads * dk)
    m3 = m_state.reshape(n_dec, 1, heads)
    c_spec = pl.BlockSpec((1, 1, hpg, dk, dv), lambda hg, b: (0, b, hg, 0, 0))
    n_spec = pl.BlockSpec((1, 1, hpg * dk), lambda hg, b: (b, 0, hg))
    o, c_new, n_new, m_new = pl.pallas_call(
        functools.partial(_mlstm_decode_kernel, hpg=hpg, heads=heads, dk=dk, dv=dv),
        grid=(n_hg, n_dec),
        in_specs=[
            pl.BlockSpec((n_dec, hpg * dk), lambda hg, b: (dec_block, hg)),
            pl.BlockSpec((n_dec, hpg * dk), lambda hg, b: (dec_block, hg)),
            pl.BlockSpec((n_dec, hpg * dv), lambda hg, b: (dec_block, hg)),
            pl.BlockSpec((n_dec, gates.shape[1]), lambda hg, b: (dec_block, 0)),
            pl.BlockSpec((n_dec, hpg * dv), lambda hg, b: (dec_block, n_hg + hg)),
            pl.BlockSpec((n_dec, hpg * dv), lambda hg, b: (dec_block, hg)),
            pl.BlockSpec((n_dec, hpg * dv), lambda hg, b: (dec_block, hg)),
            pl.BlockSpec((1, hpg * dv), lambda hg, b: (0, hg)),
            pl.BlockSpec((1, hpg * dv), lambda hg, b: (0, hg)),
            c_spec, n_spec,
            pl.BlockSpec((1, 1, heads), lambda hg, b: (b, 0, 0)),
        ],
        out_specs=[pl.BlockSpec((n_dec, hpg * dv), lambda hg, b: (dec_block, hg)), c_spec, n_spec,
                   pl.BlockSpec((1, 1, 1, hpg), lambda hg, b: (hg, b, 0, 0))],
        out_shape=[jax.ShapeDtypeStruct((q.shape[0], inner), BF16),
                   jax.ShapeDtypeStruct(c_state.shape, F32),
                   jax.ShapeDtypeStruct(n3.shape, F32),
                   jax.ShapeDtypeStruct((n_hg, n_dec, 1, hpg), F32)],
        scratch_shapes=[pltpu.VMEM((hpg * dk, n_dec), F32),
                        pltpu.VMEM((n_dec, hpg * dk), F32), pltpu.VMEM((n_dec, hpg * dk), F32)]
        + [pltpu.VMEM((n_dec, hpg * dv), F32)] * 5,
        input_output_aliases={6: 0},
        compiler_params=_cparams(("arbitrary", "arbitrary"), 48),
        name="mlstm_decode",
    )(q, k, v, gates, zo, zo, xc, gn, skip, c_state, n3, m3)
    m_new = jnp.transpose(m_new[:, :, 0, :], (1, 0, 2)).reshape(n_dec, heads)
    return o, c_new, n_new.reshape(n_state.shape), m_new.reshape(m_state.shape)


def kernel(x_prompt, x_sample, state_ret_S, state_mlstm_C, state_mlstm_n, state_mlstm_m, state_mlstm_conv,
           meta_tokens, norm_g, final_norm_g, ret_w_in, ret_w_out,
           ml_w_in, ml_conv_w, ml_conv_b, ml_w_q, ml_w_k, ml_w_v, ml_w_if, ml_b_if, ml_skip, ml_gn_g, ml_w_out):
    batch, seq, d = x_prompt.shape
    n_dec = x_sample.shape[0]
    n_tok = batch * seq
    n_chunks = seq // CHUNK
    dec0 = n_tok
    meta0 = n_tok + n_dec
    n_meta = batch * N_META
    m = -(-(meta0 + n_meta) // ROW_ALIGN) * ROW_ALIGN
    assert seq % CHUNK == 0 and n_dec == CHUNK and x_sample.shape[1] == 1
    assert state_ret_S.shape[0] == 1 and state_mlstm_C.shape[0] == 1
    dec_block = dec0 // n_dec

    ret_dk = d // RET_HEADS
    ret_dv = 2 * d // RET_HEADS
    ret_vw = RET_HEADS * ret_dv
    inner = 2 * d
    m_dk = d // M_HEADS
    m_dv = inner // M_HEADS

    ret_hpg = min(8, RET_HEADS)
    m_hpg = min(2, M_HEADS)
    in_kw = dict(bm=_pick(m, (2112, 1408, 768, 512, 384, 256)), bn=512, n_sub=4, vmem_mib=56)
    wide_kw = dict(bm=_pick(m, (1408, 768, 384, 256)), bn=256, n_sub=4, vmem_mib=56)

    x_tok = x_prompt.reshape(n_tok, d)
    h0_tail = jnp.concatenate([
        x_sample.reshape(n_dec, d),
        jnp.broadcast_to(meta_tokens[None].astype(F32), (batch, N_META, d)).reshape(n_meta, d),
        jnp.zeros((m - meta0 - n_meta, d), F32)], axis=0)

    half = ret_dk // 2
    inv = ROPE_BASE ** (-jnp.arange(half, dtype=F32) / half)
    pos = jnp.concatenate([jnp.arange(N_META + seq, dtype=jnp.int32), jnp.full((1,), PAST_LEN, jnp.int32)])
    ang = pos.astype(F32)[:, None] * inv[None, :]

    def rope_table(fn):
        t = fn(ang)
        return jnp.concatenate([
            jnp.tile(t[N_META:N_META + seq], (batch, 1)),
            jnp.broadcast_to(t[N_META + seq:], (n_dec, half)),
            jnp.tile(t[:N_META], (batch, 1)),
            jnp.broadcast_to(t[:1], (m - meta0 - n_meta, half))], axis=0)

    cos, sin = rope_table(jnp.cos), rope_table(jnp.sin)

    x0 = _rmsnorm_split(x_tok, h0_tail, norm_g[0], out_dtype=BF16, block_rows=ROW_ALIGN)
    bn = in_kw["bn"]
    qk = _matmul(x0, ret_w_in[0], n_cols=2 * d, cos=cos, sin=sin, out_dtype=BF16,
                 rope=(d // bn, ret_dk, ret_dk ** -0.5), name="ret_in_proj_qk", **in_kw)
    vg = _matmul(x0, ret_w_in[0], col0=2 * d, act=(ret_vw // bn, "id", "silu"), out_dtype=BF16,
                 name="ret_in_proj_vg", **in_kw)
    log_g = jnp.log1p(-jnp.exp2(-5.0 - jnp.arange(RET_HEADS, dtype=F32)))
    n_groups = (m - meta0) // N_META
    xr = jnp.zeros((m, ret_vw), BF16)
    xr, s_sample = _ret_decode(log_g, qk, vg, state_ret_S, xr, heads=RET_HEADS, dk=ret_dk, dv=ret_dv,
                               dec_block=dec_block, hpg=min(8, RET_HEADS))
    xr, s_meta = _ret_chunks(log_g, qk, vg, xr, None, batch=n_groups, heads=RET_HEADS, dk=ret_dk, dv=ret_dv,
                             L=N_META, n_chunks=1, row0=meta0, hpg=ret_hpg, n_real=batch, name="ret_meta")
    xr, s_prompt = _ret_chunks(log_g, qk, vg, xr, s_meta, batch=batch, heads=RET_HEADS, dk=ret_dk, dv=ret_dv,
                               L=CHUNK, n_chunks=n_chunks, row0=0, hpg=ret_hpg, name="ret_chunks")
    n_full = n_tok // wide_kw["bm"] * wide_kw["bm"]
    h0_rest = jnp.concatenate([x_tok[n_full:], h0_tail], axis=0)
    h1 = _matmul(xr, ret_w_out[0], res=(x_tok, h0_rest) if n_full else h0_rest, name="ret_out_proj", **wide_kw)

    x1 = _rmsnorm(h1, norm_g[1], out_dtype=BF16, block_rows=ROW_ALIGN)
    xm = _matmul(x1, ml_w_in[0], n_cols=inner, name="ml_in_proj_x", **in_kw)
    zo = _matmul(x1, ml_w_in[0], col0=inner, act=(inner // bn, "silu", "sigmoid"), name="ml_in_proj_zo", **in_kw)

    xm_meta = xm[meta0:meta0 + n_meta].reshape(batch, N_META, inner)
    conv_state = state_mlstm_conv[0]
    pad_rows = jnp.zeros((m - meta0 - n_meta, inner), F32)

    def tail_prev(kk):
        meta_prev = jnp.pad(xm_meta, ((0, 0), (kk, 0), (0, 0)))[:, :N_META].reshape(n_meta, inner)
        return jnp.concatenate([conv_state[:, CONV_W - 1 - kk], meta_prev, pad_rows], axis=0)

    x_tail = tuple(tail_prev(kk) for kk in range(1, CONV_W))
    xc, v = _conv_v(xm, x_tail, ml_conv_w[0], ml_conv_b[0].reshape(1, inner), _v_blockdiag(ml_w_v[0]),
                    n_tok=n_tok, seq=seq, meta0=meta0, inner=inner, ct=min(inner, 2048))

    q = _matmul(xc, ml_w_q[0], out_dtype=BF16, name="ml_q_proj", **wide_kw)
    k = _matmul(xc, ml_w_k[0], out_dtype=BF16, name="ml_k_proj", **wide_kw)
    gates, gates_t = _gates(q, k, v, ml_w_if[0], ml_b_if[0], heads=M_HEADS, bm=ROW_ALIGN)

    gn = ml_gn_g[0].reshape(1, inner)
    skip = ml_skip[0].reshape(1, inner)
    xc, c_sample, n_sample, m_sample = _mlstm_decode(
        q, k, v, gates, zo, xc, gn, skip, state_mlstm_C, state_mlstm_n, state_mlstm_m,
        heads=M_HEADS, dk=m_dk, dv=m_dv, dec_block=dec_block, hpg=min(2, M_HEADS))
    gt_meta = gates_t[:, meta0:].reshape(2 * M_HEADS, n_groups, N_META).transpose(1, 0, 2)
    xc, c_meta, n_meta_s, m_meta = _mlstm_chunks(
        q, k, v, gates, gt_meta, zo, xc, gn, skip, None,
        batch=n_groups, heads=M_HEADS, dk=m_dk, dv=m_dv, L=N_META, n_chunks=1, row0=meta0, hpg=m_hpg,
        n_real=batch, name="mlstm_meta")
    xc, c_prompt, n_prompt, m_prompt = _mlstm_chunks(
        q, k, v, gates, gates_t, zo, xc, gn, skip, (c_meta, n_meta_s, m_meta),
        batch=batch, heads=M_HEADS, dk=m_dk, dv=m_dv, L=CHUNK, n_chunks=n_chunks, row0=0, hpg=m_hpg,
        name="mlstm_chunks")
    h2 = _matmul(xc, ml_w_out[0], res=h1, name="ml_out_proj", **wide_kw)

    y_prompt = _rmsnorm(h2, final_norm_g, out_dtype=F32, block_rows=CHUNK, n_blocks=n_tok // CHUNK)
    y_sample = _rmsnorm(h2, final_norm_g, out_dtype=F32, block_rows=CHUNK, first_block=dec_block, n_blocks=1)
    conv_prompt = jnp.stack([xm[(b + 1) * seq - (CONV_W - 1):(b + 1) * seq] for b in range(batch)])
    conv_sample = jnp.concatenate([conv_state[:, 1:], xm[dec0:dec0 + n_dec, None]], axis=1)
    return (y_prompt.reshape(batch, seq, d), y_sample.reshape(n_dec, 1, d),
            s_prompt[None], s_sample,
            c_prompt[None], c_sample,
            n_prompt.reshape(1, batch, M_HEADS, m_dk), n_sample,
            m_prompt.reshape(1, batch, M_HEADS), m_sample,
            conv_prompt[None], conv_sample[None])
```
